```python
import math
import jax, jax.numpy as jnp
from jax import lax
import numpy as np

D_MODEL = 1024
BATCH = 8
SEQ = 4096
DEPTH = 2

HEAD_DIM = 64
N_HEADS_A = D_MODEL // HEAD_DIM
N_HEADS_B = D_MODEL // HEAD_DIM
N_KV_B = 4
GROUP_B = N_HEADS_B // N_KV_B
D_FF = 2816
DILATED_PATTERNS = ((128, 1), (512, 4), (2048, 16))
WINDOW_B = 128
BLOCK = 128
N_A_LAYERS = DEPTH // 2
N_B_LAYERS = DEPTH - N_A_LAYERS
ALPHA = (2.0 * DEPTH) ** 0.25
BETA = (8.0 * DEPTH) ** -0.25
LN_EPS = 1e-5

kernel_name = "yoco_dilated_swa_sink_hybrid"


def alibi_slopes(n):
    return np.array([2.0 ** (-8.0 * (h + 1) / n) for h in range(n)], dtype=np.float32)


def layer_norm(x, g, b):
    xf = x.astype(jnp.float32)
    mu = xf.mean(-1, keepdims=True)
    var = jnp.mean(jnp.square(xf - mu), -1, keepdims=True)
    y = (xf - mu) * lax.rsqrt(var + LN_EPS) * g.astype(jnp.float32) + b.astype(jnp.float32)
    return y.astype(x.dtype)


def swiglu(x, w_in, w_out):
    gate, up = jnp.split(x @ w_in, 2, axis=-1)
    return (jax.nn.silu(gate) * up) @ w_out


def banded_attention(q, k, v, slopes, max_dist, dist_scale, sinks=None):
    b, L, hk, g, dh = q.shape
    P = BLOCK
    nb = -(-L // P)
    pad = nb * P - L
    if pad:
        q = jnp.pad(q, ((0, 0), (0, pad), (0, 0), (0, 0), (0, 0)))
        k = jnp.pad(k, ((0, 0), (0, pad), (0, 0), (0, 0)))
        v = jnp.pad(v, ((0, 0), (0, pad), (0, 0), (0, 0)))
    qb = q.reshape(b, nb, P, hk, g, dh)

    def with_prev(t):
        t = t.reshape(b, nb, P, hk, dh)
        prev = jnp.concatenate([jnp.zeros_like(t[:, :1]), t[:, :-1]], axis=1)
        return jnp.concatenate([prev, t], axis=2)

    kc, vc = with_prev(k), with_prev(v)
    s = jnp.einsum('bnqhgd,bnkhd->bnhgqk', qb.astype(jnp.float32), kc.astype(jnp.float32)) * (dh ** -0.5)
    qi = np.arange(P)[:, None]
    kj = np.arange(2 * P)[None, :]
    dist = P + qi - kj
    kpos = (np.arange(nb)[:, None, None] - 1) * P + kj[None]
    valid = (dist >= 0) & (dist <= max_dist) & (kpos >= 0)
    bias = -(slopes.astype(jnp.float32)[:, :, None, None]
             * jnp.asarray((dist * dist_scale).astype(np.float32)))
    s = jnp.where(jnp.asarray(valid)[None, :, None, None], s + bias, -jnp.inf)
    m = s.max(-1, keepdims=True)
    if sinks is not None:
        sink = sinks.astype(jnp.float32)[:, :, None, None]
        m = jnp.maximum(m, sink)
    p = jnp.exp(s - m)
    den = p.sum(-1, keepdims=True)
    if sinks is not None:
        den = den + jnp.exp(sink - m)
    o = jnp.einsum('bnhgqk,bnkhd->bnqhgd', p / den, vc.astype(jnp.float32))
    lse = jnp.moveaxis((m + jnp.log(den))[..., 0], -1, 2)
    o = o.reshape(b, nb * P, hk, g, dh)[:, :L].astype(q.dtype)
    lse = lse.reshape(b, nb * P, hk, g)[:, :L]
    return o, lse


def to_strided(t, d):
    b, S = t.shape[:2]
    t = t.reshape(b, S // d, d, *t.shape[2:])
    t = jnp.moveaxis(t, 2, 1)
    return t.reshape(b * d, S // d, *t.shape[3:])


def from_strided(t, b, d):
    t = t.reshape(b, d, *t.shape[1:])
    t = jnp.moveaxis(t, 1, 2)
    return t.reshape(b, t.shape[1] * d, *t.shape[3:])


def dilated_mixer(h, w_qkv, w_o):
    b, S, _ = h.shape
    q, k, v = jnp.split(h @ w_qkv, 3, axis=-1)
    q = q.reshape(b, S, N_HEADS_A, 1, HEAD_DIM)
    k = k.reshape(b, S, N_HEADS_A, HEAD_DIM)
    v = v.reshape(b, S, N_HEADS_A, HEAD_DIM)
    slopes = jnp.asarray(alibi_slopes(N_HEADS_A)).reshape(N_HEADS_A, 1)
    outs, lses = [], []
    for window, d in DILATED_PATTERNS:
        o, lse = banded_attention(to_strided(q, d), to_strided(k, d), to_strided(v, d),
                                  slopes, window // d, d)
        outs.append(from_strided(o, b, d))
        lses.append(from_strided(lse, b, d))
    wts = jax.nn.softmax(jnp.stack(lses, 0), axis=0)
    out = jnp.sum(wts[..., None] * jnp.stack(outs, 0).astype(jnp.float32), axis=0)
    return out.astype(h.dtype).reshape(b, S, D_MODEL) @ w_o


def swa_sink_mixer(h, k_sh, v_sh, w_q, sinks, w_o):
    b, S, _ = h.shape
    q = (h @ w_q).reshape(b, S, N_KV_B, GROUP_B, HEAD_DIM)
    slopes = jnp.asarray(alibi_slopes(N_HEADS_B)).reshape(N_KV_B, GROUP_B)
    o, _ = banded_attention(q, k_sh, v_sh, slopes, WINDOW_B - 1, 1,
                            sinks.reshape(N_KV_B, GROUP_B))
    return o.reshape(b, S, D_MODEL) @ w_o


def setup_inputs(seed: int = 0) -> dict:
    key = jax.random.key(seed)
    ks = jax.random.split(key, 16)
    f32 = jnp.float32
    nrm = lambda k, shape, fan_in, scale=1.0: jax.random.normal(k, shape, f32) * (fan_in ** -0.5) * scale
    return {
        "x": jax.random.normal(ks[0], (BATCH, SEQ, D_MODEL), f32),
        "ffn1_w_in": nrm(ks[1], (DEPTH, D_MODEL, 2 * D_FF), D_MODEL),
        "ffn1_w_out": nrm(ks[2], (DEPTH, D_FF, D_MODEL), D_FF, BETA),
        "ffn2_w_in": nrm(ks[3], (DEPTH, D_MODEL, 2 * D_FF), D_MODEL),
        "ffn2_w_out": nrm(ks[4], (DEPTH, D_FF, D_MODEL), D_FF, BETA),
        "ln_g": 1.0 + 0.02 * jax.random.normal(ks[5], (DEPTH, 3, D_MODEL), f32),
        "ln_b": 0.02 * jax.random.normal(ks[6], (DEPTH, 3, D_MODEL), f32),
        "a_w_qkv": nrm(ks[7], (N_A_LAYERS, D_MODEL, 3 * N_HEADS_A * HEAD_DIM), D_MODEL),
        "a_w_o": nrm(ks[8], (N_A_LAYERS, N_HEADS_A * HEAD_DIM, D_MODEL), D_MODEL, BETA),
        "kv_w": nrm(ks[9], (D_MODEL, 2 * N_KV_B * HEAD_DIM), D_MODEL),
        "b_w_q": nrm(ks[10], (N_B_LAYERS, D_MODEL, N_HEADS_B * HEAD_DIM), D_MODEL),
        "b_sinks": 0.5 * jax.random.normal(ks[11], (N_B_LAYERS, N_HEADS_B), f32),
        "b_w_o": nrm(ks[12], (N_B_LAYERS, N_HEADS_B * HEAD_DIM, D_MODEL), D_MODEL, BETA),
    }


def reference(x, ffn1_w_in, ffn1_w_out, ffn2_w_in, ffn2_w_out, ln_g, ln_b,
              a_w_qkv, a_w_o, kv_w, b_w_q, b_sinks, b_w_o):
    b, S, _ = x.shape
    k_sh = v_sh = None
    for i in range(DEPTH):
        x = layer_norm(ALPHA * x + 0.5 * swiglu(x, ffn1_w_in[i], ffn1_w_out[i]), ln_g[i, 0], ln_b[i, 0])
        if i < N_A_LAYERS:
            mix = dilated_mixer(x, a_w_qkv[i], a_w_o[i])
        else:
            j = i - N_A_LAYERS
            mix = swa_sink_mixer(x, k_sh, v_sh, b_w_q[j], b_sinks[j], b_w_o[j])
        x = layer_norm(ALPHA * x + mix, ln_g[i, 1], ln_b[i, 1])
        x = layer_norm(ALPHA * x + 0.5 * swiglu(x, ffn2_w_in[i], ffn2_w_out[i]), ln_g[i, 2], ln_b[i, 2])
        if i == N_A_LAYERS - 1:
            k_flat, v_flat = jnp.split(x @ kv_w, 2, axis=-1)
            k_sh = k_flat.reshape(b, S, N_KV_B, HEAD_DIM)
            v_sh = v_flat.reshape(b, S, N_KV_B, HEAD_DIM)
    return x
```

```python
import functools

import numpy as np
import jax
import jax.numpy as jnp
from jax import lax
from jax.experimental import pallas as pl
from jax.experimental.pallas import tpu as pltpu

D_MODEL = 1024
DEPTH = 2
HEAD_DIM = 64
N_HEADS = D_MODEL // HEAD_DIM
N_KV_B = 4
GROUP_B = N_HEADS // N_KV_B
D_FF = 2816
DILATED_PATTERNS = ((128, 1), (512, 4), (2048, 16))
WINDOW_B = 128
BLOCK = 128
N_A_LAYERS = DEPTH // 2
ALPHA = (2.0 * DEPTH) ** 0.25
LN_EPS = 1e-5

LANES = 128
HEADS_PER_BLOCK = LANES // HEAD_DIM
MASK_VALUE = -1e30
VMEM_LIMIT = 56 * 1024 * 1024

BF16 = jnp.bfloat16
F32 = jnp.float32


def _layer_norm(z, g, b):
    mu = jnp.mean(z, axis=-1, keepdims=True)
    zc = z - mu
    var = jnp.mean(zc * zc, axis=-1, keepdims=True)
    return zc * lax.rsqrt(var + LN_EPS) * g + b


def _ffn_ln_kernel(x_ref, win_ref, wout_ref, g_ref, b_ref, o_ref):
    x = x_ref[...]
    h = jnp.dot(x.astype(BF16), win_ref[...], preferred_element_type=F32)
    gate = h[:, :D_FF]
    up = h[:, D_FF:]
    act = gate * jax.nn.sigmoid(gate) * up
    y = jnp.dot(act.astype(BF16), wout_ref[...], preferred_element_type=F32)
    z = ALPHA * x + 0.5 * y
    o_ref[...] = _layer_norm(z, g_ref[...], b_ref[...])


def _resident(shape):
    return pl.BlockSpec(shape, lambda *_: (0,) * len(shape), pipeline_mode=pl.Buffered(1))


def _ffn_ln(x, w_in, w_out, g, b, *, tm=512):
    t, d = x.shape
    return pl.pallas_call(
        _ffn_ln_kernel,
        grid=(t // tm,),
        in_specs=[
            pl.BlockSpec((tm, d), lambda i: (i, 0)),
            _resident(w_in.shape),
            _resident(w_out.shape),
            _resident((1, d)),
            _resident((1, d)),
        ],
        out_specs=pl.BlockSpec((tm, d), lambda i: (i, 0)),
        out_shape=jax.ShapeDtypeStruct((t, d), F32),
        compiler_params=pltpu.CompilerParams(
            dimension_semantics=("arbitrary",), vmem_limit_bytes=VMEM_LIMIT),
        name="ffn_ln",
    )(x, w_in, w_out, g.reshape(1, d), b.reshape(1, d))


def _proj_kernel(x_ref, w_ref, o_ref):
    o_ref[...] = jnp.dot(x_ref[...].astype(BF16), w_ref[...],
                         preferred_element_type=F32).astype(o_ref.dtype)


def _proj(x, w, *, tm=512, out_dtype=F32):
    t, d = x.shape
    n = w.shape[1]
    return pl.pallas_call(
        _proj_kernel,
        grid=(t // tm,),
        in_specs=[pl.BlockSpec((tm, d), lambda i: (i, 0)), _resident(w.shape)],
        out_specs=pl.BlockSpec((tm, n), lambda i: (i, 0)),
        out_shape=jax.ShapeDtypeStruct((t, n), out_dtype),
        compiler_params=pltpu.CompilerParams(
            dimension_semantics=("arbitrary",), vmem_limit_bytes=VMEM_LIMIT),
        name="proj",
    )(x, w)


def _oproj_ln_kernel(a_ref, x_ref, w_ref, g_ref, b_ref, o_ref):
    mix = jnp.dot(a_ref[...].astype(BF16), w_ref[...], preferred_element_type=F32)
    z = ALPHA * x_ref[...] + mix
    o_ref[...] = _layer_norm(z, g_ref[...], b_ref[...])


def _oproj_ln(a, x, w, g, b, *, tm=512):
    t, d = x.shape
    return pl.pallas_call(
        _oproj_ln_kernel,
        grid=(t // tm,),
        in_specs=[
            pl.BlockSpec((tm, a.shape[1]), lambda i: (i, 0)),
            pl.BlockSpec((tm, d), lambda i: (i, 0)),
            _resident(w.shape),
            _resident((1, d)),
            _resident((1, d)),
        ],
        out_specs=pl.BlockSpec((tm, d), lambda i: (i, 0)),
        out_shape=jax.ShapeDtypeStruct((t, d), F32),
        compiler_params=pltpu.CompilerParams(
            dimension_semantics=("arbitrary",), vmem_limit_bytes=VMEM_LIMIT),
        name="oproj_ln",
    )(a, x, w, g.reshape(1, d), b.reshape(1, d))


def _rows(start, stride):
    if stride == 1:
        return pl.ds(pl.multiple_of(start, BLOCK), BLOCK)
    return pl.ds(start, BLOCK, stride=stride)


def _attn_kernel(slope_ref, sink_ref, q_ref, k_ref, v_ref, o_ref, bias_ref, *scratch,
                 patterns, use_sinks):
    seq = q_ref.shape[0]
    blk = pl.program_id(1)
    multi = len(patterns) > 1
    if multi:
        acc_ref, lse_ref = scratch

    qi = lax.broadcasted_iota(jnp.int32, (BLOCK, 2 * BLOCK), 0)
    kj = lax.broadcasted_iota(jnp.int32, (BLOCK, 2 * BLOCK), 1)
    dist = BLOCK + qi - kj
    dist_f = dist.astype(F32)
    lane = lax.broadcasted_iota(jnp.int32, (BLOCK, LANES), 1)
    head_masks = [(lane >= h * HEAD_DIM) & (lane < (h + 1) * HEAD_DIM)
                  for h in range(HEADS_PER_BLOCK)]

    for pi, (_, max_dist, scale) in enumerate(patterns):
        valid = (dist >= 0) & (dist <= max_dist)
        for h in range(HEADS_PER_BLOCK):
            slope = slope_ref[blk * HEADS_PER_BLOCK + h]
            bias = jnp.where(valid, (-slope * scale) * dist_f, MASK_VALUE)
            bias_ref[(pi * 2 + h) * 2 + 0] = bias
            bias_ref[(pi * 2 + h) * 2 + 1] = jnp.where(kj >= BLOCK, bias, MASK_VALUE)

    for pi, (stride, _, _) in enumerate(patterns):
        blocks_per_class = seq // (stride * BLOCK)
        shift = blocks_per_class.bit_length() - 1
        assert blocks_per_class == 1 << shift

        def task(t, carry, pi=pi, stride=stride, shift=shift, blocks_per_class=blocks_per_class):
            r = lax.shift_right_logical(t, shift)
            n = lax.bitwise_and(t, blocks_per_class - 1)
            start = r + n * (BLOCK * stride)
            prev = r + jnp.maximum(n - 1, 0) * (BLOCK * stride)
            first = (n == 0).astype(jnp.int32)
            cur_rows = _rows(start, stride)
            prev_rows = _rows(prev, stride)
            q = q_ref[cur_rows, :] * (HEAD_DIM ** -0.5)
            kk = jnp.concatenate([k_ref[prev_rows, :], k_ref[cur_rows, :]], axis=0).astype(BF16)
            vv = jnp.concatenate([v_ref[prev_rows, :], v_ref[cur_rows, :]], axis=0).astype(BF16)
            outs, lses = [], []
            for h in range(HEADS_PER_BLOCK):
                qh = jnp.where(head_masks[h], q, 0.0).astype(BF16)
                s = lax.dot_general(qh, kk, (((1,), (1,)), ((), ())),
                                    preferred_element_type=F32)
                s = s + bias_ref[(pi * 2 + h) * 2 + first]
                m = jnp.max(s, axis=-1, keepdims=True)
                if use_sinks:
                    sink = sink_ref[blk * HEADS_PER_BLOCK + h]
                    m = jnp.maximum(m, sink)
                p = jnp.exp(s - m)
                den = jnp.sum(p, axis=-1, keepdims=True)
                if use_sinks:
                    den = den + jnp.exp(sink - m)
                o_h = jnp.dot(p.astype(BF16), vv, preferred_element_type=F32)
                outs.append(o_h / den)
                if multi:
                    lses.append(m + jnp.log(den))
            o = jnp.where(head_masks[0], outs[0], outs[1])
            if multi:
                lse = jnp.where(head_masks[0], lses[0], lses[1])
                acc_ref[pi, cur_rows, :] = o
                lse_ref[pi, cur_rows, :] = lse
            else:
                o_ref[cur_rows, :] = o.astype(o_ref.dtype)
            return carry

        lax.fori_loop(0, seq // BLOCK, task, 0)

    if multi:
        chunk = 256

        def combine(i, carry):
            rows = pl.ds(pl.multiple_of(i * chunk, chunk), chunk)
            ls = [lse_ref[pi, rows, :] for pi in range(len(patterns))]
            mx = functools.reduce(jnp.maximum, ls)
            es = [jnp.exp(l - mx) for l in ls]
            tot = functools.reduce(lambda a, b: a + b, es)
            num = functools.reduce(lambda a, b: a + b,
                                   [e * acc_ref[pi, rows, :] for pi, e in enumerate(es)])
            o_ref[rows, :] = (num / tot).astype(o_ref.dtype)
            return carry

        lax.fori_loop(0, seq // chunk, combine, 0)


def _attention(q_arr, kv_arr, slopes, sinks, *, q_block, k_block, v_block, patterns, use_sinks):
    b, seq, _ = q_arr.shape
    n_blocks = N_HEADS // HEADS_PER_BLOCK
    n_pat = len(patterns)
    scratch = [pltpu.VMEM((n_pat * 4, BLOCK, 2 * BLOCK), F32)]
    if n_pat > 1:
        scratch += [pltpu.VMEM((n_pat, seq, LANES), F32), pltpu.VMEM((n_pat, seq, LANES), F32)]
    smem = pl.BlockSpec(memory_space=pltpu.SMEM)
    return pl.pallas_call(
        functools.partial(_attn_kernel, patterns=patterns, use_sinks=use_sinks),
        grid=(b, n_blocks),
        in_specs=[
            smem, smem,
            pl.BlockSpec((None, seq, LANES), lambda i, j: (i, 0, q_block(j))),
            pl.BlockSpec((None, seq, LANES), lambda i, j: (i, 0, k_block(j))),
            pl.BlockSpec((None, seq, LANES), lambda i, j: (i, 0, v_block(j))),
        ],
        out_specs=pl.BlockSpec((None, seq, LANES), lambda i, j: (i, 0, j)),
        out_shape=jax.ShapeDtypeStruct((b, seq, D_MODEL), BF16),
        scratch_shapes=scratch,
        compiler_params=pltpu.CompilerParams(
            dimension_semantics=("arbitrary", "arbitrary"), vmem_limit_bytes=VMEM_LIMIT),
        name="attn_dilated" if n_pat > 1 else "attn_swa",
    )(slopes, sinks, q_arr, kv_arr, kv_arr)


def _alibi_slopes(n):
    return np.array([2.0 ** (-8.0 * (h + 1) / n) for h in range(n)], dtype=np.float32)


def _swa_head_order():
    order = []
    for c in range(N_KV_B // HEADS_PER_BLOCK):
        for g in range(GROUP_B):
            order += [(HEADS_PER_BLOCK * c) * GROUP_B + g, (HEADS_PER_BLOCK * c + 1) * GROUP_B + g]
    return np.array(order, dtype=np.int32)


def kernel(x, ffn1_w_in, ffn1_w_out, ffn2_w_in, ffn2_w_out, ln_g, ln_b,
           a_w_qkv, a_w_o, kv_w, b_w_q, b_sinks, b_w_o):
    b, seq, d = x.shape
    t = b * seq
    xs = x.reshape(t, d)
    slopes = _alibi_slopes(N_HEADS)
    no_sinks = jnp.zeros((N_HEADS,), F32)
    dil_patterns = tuple((dil, win // dil, float(dil)) for win, dil in DILATED_PATTERNS)
    swa_patterns = ((1, WINDOW_B - 1, 1.0),)
    n_blocks = N_HEADS // HEADS_PER_BLOCK
    head_order = _swa_head_order()
    col_order = (head_order[:, None] * HEAD_DIM + np.arange(HEAD_DIM)[None, :]).reshape(-1)
    kv_blocks = N_KV_B // HEADS_PER_BLOCK
    kv = None
    for i in range(DEPTH):
        xs = _ffn_ln(xs, ffn1_w_in[i].astype(BF16), ffn1_w_out[i].astype(BF16),
                     ln_g[i, 0], ln_b[i, 0])
        if i < N_A_LAYERS:
            qkv = _proj(xs, a_w_qkv[i].astype(BF16)).reshape(b, seq, 3 * d)
            att = _attention(
                qkv, qkv, jnp.asarray(slopes), no_sinks,
                q_block=lambda j: j, k_block=lambda j: n_blocks + j,
                v_block=lambda j: 2 * n_blocks + j,
                patterns=dil_patterns, use_sinks=False)
            w_o = a_w_o[i].astype(BF16)
        else:
            j = i - N_A_LAYERS
            q = _proj(xs, b_w_q[j][:, col_order].astype(BF16)).reshape(b, seq, d)
            att = _attention(
                q, kv, jnp.asarray(slopes[head_order]), b_sinks[j][head_order],
                q_block=lambda jb: jb, k_block=lambda jb: jb // GROUP_B,
                v_block=lambda jb: kv_blocks + jb // GROUP_B,
                patterns=swa_patterns, use_sinks=True)
            w_o = b_w_o[j][col_order, :].astype(BF16)
        xs = _oproj_ln(att.reshape(t, d), xs, w_o, ln_g[i, 1], ln_b[i, 1])
        xs = _ffn_ln(xs, ffn2_w_in[i].astype(BF16), ffn2_w_out[i].astype(BF16),
                     ln_g[i, 2], ln_b[i, 2])
        if i == N_A_LAYERS - 1:
            kv = _proj(xs, kv_w.astype(BF16)).reshape(b, seq, 2 * N_KV_B * HEAD_DIM)
    return xs.reshape(b, seq, d)
```

```python
import functools

import numpy as np
import jax
import jax.numpy as jnp
from jax import lax
from jax.experimental import pallas as pl
from jax.experimental.pallas import tpu as pltpu

D_MODEL = 1024
DEPTH = 2
HEAD_DIM = 64
N_HEADS = D_MODEL // HEAD_DIM
N_KV_B = 4
GROUP_B = N_HEADS // N_KV_B
D_FF = 2816
DILATED_PATTERNS = ((128, 1), (512, 4), (2048, 16))
WINDOW_B = 128
BLOCK = 128
N_A_LAYERS = DEPTH // 2
ALPHA = (2.0 * DEPTH) ** 0.25
LN_EPS = 1e-5

LANES = 128
HEADS_PER_BLOCK = LANES // HEAD_DIM
MASK_VALUE = -1e30
VMEM_LIMIT = 56 * 1024 * 1024
TASK_UNROLL = 8

BF16 = jnp.bfloat16
F32 = jnp.float32


def _layer_norm(z, g, b):
    mu = jnp.mean(z, axis=-1, keepdims=True)
    zc = z - mu
    var = jnp.mean(zc * zc, axis=-1, keepdims=True)
    return zc * lax.rsqrt(var + LN_EPS) * g + b


def _ffn_ln_kernel(x_ref, win_ref, wout_ref, g_ref, b_ref, o_ref):
    x = x_ref[...]
    h = jnp.dot(x.astype(BF16), win_ref[...], preferred_element_type=F32)
    gate = h[:, :D_FF]
    up = h[:, D_FF:]
    act = gate * jax.nn.sigmoid(gate) * up
    y = jnp.dot(act.astype(BF16), wout_ref[...], preferred_element_type=F32)
    z = ALPHA * x + 0.5 * y
    o_ref[...] = _layer_norm(z, g_ref[...], b_ref[...])


def _resident(shape):
    return pl.BlockSpec(shape, lambda *_: (0,) * len(shape), pipeline_mode=pl.Buffered(1))


def _ffn_ln(x, w_in, w_out, g, b, *, tm=512):
    t, d = x.shape
    return pl.pallas_call(
        _ffn_ln_kernel,
        grid=(t // tm,),
        in_specs=[
            pl.BlockSpec((tm, d), lambda i: (i, 0)),
            _resident(w_in.shape),
            _resident(w_out.shape),
            _resident((1, d)),
            _resident((1, d)),
        ],
        out_specs=pl.BlockSpec((tm, d), lambda i: (i, 0)),
        out_shape=jax.ShapeDtypeStruct((t, d), F32),
        compiler_params=pltpu.CompilerParams(
            dimension_semantics=("arbitrary",), vmem_limit_bytes=VMEM_LIMIT),
        name="ffn_ln",
    )(x, w_in, w_out, g.reshape(1, d), b.reshape(1, d))


def _proj_kernel(x_ref, w_ref, o_ref):
    o_ref[...] = jnp.dot(x_ref[...].astype(BF16), w_ref[...],
                         preferred_element_type=F32).astype(o_ref.dtype)


def _proj(x, w, *, tm=512, out_dtype=F32):
    t, d = x.shape
    n = w.shape[1]
    return pl.pallas_call(
        _proj_kernel,
        grid=(t // tm,),
        in_specs=[pl.BlockSpec((tm, d), lambda i: (i, 0)), _resident(w.shape)],
        out_specs=pl.BlockSpec((tm, n), lambda i: (i, 0)),
        out_shape=jax.ShapeDtypeStruct((t, n), out_dtype),
        compiler_params=pltpu.CompilerParams(
            dimension_semantics=("arbitrary",), vmem_limit_bytes=VMEM_LIMIT),
        name="proj",
    )(x, w)


def _oproj_ln_kernel(a_ref, x_ref, w_ref, g_ref, b_ref, o_ref):
    mix = jnp.dot(a_ref[...].astype(BF16), w_ref[...], preferred_element_type=F32)
    z = ALPHA * x_ref[...] + mix
    o_ref[...] = _layer_norm(z, g_ref[...], b_ref[...])


def _oproj_ln(a, x, w, g, b, *, tm=512):
    t, d = x.shape
    return pl.pallas_call(
        _oproj_ln_kernel,
        grid=(t // tm,),
        in_specs=[
            pl.BlockSpec((tm, a.shape[1]), lambda i: (i, 0)),
            pl.BlockSpec((tm, d), lambda i: (i, 0)),
            _resident(w.shape),
            _resident((1, d)),
            _resident((1, d)),
        ],
        out_specs=pl.BlockSpec((tm, d), lambda i: (i, 0)),
        out_shape=jax.ShapeDtypeStruct((t, d), F32),
        compiler_params=pltpu.CompilerParams(
            dimension_semantics=("arbitrary",), vmem_limit_bytes=VMEM_LIMIT),
        name="oproj_ln",
    )(a, x, w, g.reshape(1, d), b.reshape(1, d))


def _rows(start, stride):
    if stride == 1:
        return pl.ds(pl.multiple_of(start, BLOCK), BLOCK)
    return pl.ds(start, BLOCK, stride=stride)


def _attn_kernel(slope_ref, sink_ref, q_ref, k_ref, v_ref, o_ref, bias_ref, *scratch,
                 patterns, use_sinks):
    seq = q_ref.shape[0]
    blk = pl.program_id(1)
    multi = len(patterns) > 1
    if multi:
        acc_ref, lse_ref = scratch

    qi = lax.broadcasted_iota(jnp.int32, (BLOCK, 2 * BLOCK), 0)
    kj = lax.broadcasted_iota(jnp.int32, (BLOCK, 2 * BLOCK), 1)
    dist = BLOCK + qi - kj
    dist_f = dist.astype(F32)
    lane = lax.broadcasted_iota(jnp.int32, (BLOCK, LANES), 1)
    head_masks = [(lane >= h * HEAD_DIM) & (lane < (h + 1) * HEAD_DIM)
                  for h in range(HEADS_PER_BLOCK)]

    for pi, (_, max_dist, scale) in enumerate(patterns):
        valid = (dist >= 0) & (dist <= max_dist)
        for h in range(HEADS_PER_BLOCK):
            slope = slope_ref[blk * HEADS_PER_BLOCK + h]
            bias = jnp.where(valid, (-slope * scale) * dist_f, MASK_VALUE)
            bias_ref[(pi * 2 + h) * 2 + 0] = bias
            bias_ref[(pi * 2 + h) * 2 + 1] = jnp.where(kj >= BLOCK, bias, MASK_VALUE)

    for pi, (stride, _, _) in enumerate(patterns):
        blocks_per_class = seq // (stride * BLOCK)
        shift = blocks_per_class.bit_length() - 1
        assert blocks_per_class == 1 << shift

        def task(t, carry, pi=pi, stride=stride, shift=shift, blocks_per_class=blocks_per_class):
            r = lax.shift_right_logical(t, shift)
            n = lax.bitwise_and(t, blocks_per_class - 1)
            start = r + n * (BLOCK * stride)
            prev = r + jnp.maximum(n - 1, 0) * (BLOCK * stride)
            first = (n == 0).astype(jnp.int32)
            cur_rows = _rows(start, stride)
            prev_rows = _rows(prev, stride)
            q = q_ref[cur_rows, :] * (HEAD_DIM ** -0.5)
            kk = jnp.concatenate([k_ref[prev_rows, :], k_ref[cur_rows, :]], axis=0).astype(BF16)
            vv = jnp.concatenate([v_ref[prev_rows, :], v_ref[cur_rows, :]], axis=0).astype(BF16)
            outs, lses = [], []
            for h in range(HEADS_PER_BLOCK):
                qh = jnp.where(head_masks[h], q, 0.0).astype(BF16)
                s = lax.dot_general(qh, kk, (((1,), (1,)), ((), ())),
                                    preferred_element_type=F32)
                s = s + bias_ref[(pi * 2 + h) * 2 + first]
                m = jnp.max(s, axis=-1, keepdims=True)
                if use_sinks:
                    sink = sink_ref[blk * HEADS_PER_BLOCK + h]
                    m = jnp.maximum(m, sink)
                p = jnp.exp(s - m)
                den = jnp.sum(p, axis=-1, keepdims=True)
                if use_sinks:
                    den = den + jnp.exp(sink - m)
                o_h = jnp.dot(p.astype(BF16), vv, preferred_element_type=F32)
                outs.append(o_h / den)
                if multi:
                    lses.append(m + jnp.log(den))
            o = jnp.where(head_masks[0], outs[0], outs[1])
            if multi:
                lse = jnp.where(head_masks[0], lses[0], lses[1])
                acc_ref[pi, cur_rows, :] = o
                lse_ref[pi, cur_rows, :] = lse
            else:
                o_ref[cur_rows, :] = o.astype(o_ref.dtype)
            return carry

        lax.fori_loop(0, seq // BLOCK, task, 0, unroll=TASK_UNROLL)

    if multi:
        chunk = 256

        def combine(i, carry):
            rows = pl.ds(pl.multiple_of(i * chunk, chunk), chunk)
            ls = [lse_ref[pi, rows, :] for pi in range(len(patterns))]
            mx = functools.reduce(jnp.maximum, ls)
            es = [jnp.exp(l - mx) for l in ls]
            tot = functools.reduce(lambda a, b: a + b, es)
            num = functools.reduce(lambda a, b: a + b,
                                   [e * acc_ref[pi, rows, :] for pi, e in enumerate(es)])
            o_ref[rows, :] = (num / tot).astype(o_ref.dtype)
            return carry

        lax.fori_loop(0, seq // chunk, combine, 0)


def _attention(q_arr, kv_arr, slopes, sinks, *, q_block, k_block, v_block, patterns, use_sinks):
    b, seq, _ = q_arr.shape
    n_blocks = N_HEADS // HEADS_PER_BLOCK
    n_pat = len(patterns)
    scratch = [pltpu.VMEM((n_pat * 4, BLOCK, 2 * BLOCK), F32)]
    if n_pat > 1:
        scratch += [pltpu.VMEM((n_pat, seq, LANES), F32), pltpu.VMEM((n_pat, seq, LANES), F32)]
    smem = pl.BlockSpec(memory_space=pltpu.SMEM)
    return pl.pallas_call(
        functools.partial(_attn_kernel, patterns=patterns, use_sinks=use_sinks),
        grid=(b, n_blocks),
        in_specs=[
            smem, smem,
            pl.BlockSpec((None, seq, LANES), lambda i, j: (i, 0, q_block(j))),
            pl.BlockSpec((None, seq, LANES), lambda i, j: (i, 0, k_block(j))),
            pl.BlockSpec((None, seq, LANES), lambda i, j: (i, 0, v_block(j))),
        ],
        out_specs=pl.BlockSpec((None, seq, LANES), lambda i, j: (i, 0, j)),
        out_shape=jax.ShapeDtypeStruct((b, seq, D_MODEL), BF16),
        scratch_shapes=scratch,
        compiler_params=pltpu.CompilerParams(
            dimension_semantics=("arbitrary", "arbitrary"), vmem_limit_bytes=VMEM_LIMIT),
        name="attn_dilated" if n_pat > 1 else "attn_swa",
    )(slopes, sinks, q_arr, kv_arr, kv_arr)


def _alibi_slopes(n):
    return np.array([2.0 ** (-8.0 * (h + 1) / n) for h in range(n)], dtype=np.float32)


def _swa_head_order():
    order = []
    for c in range(N_KV_B // HEADS_PER_BLOCK):
        for g in range(GROUP_B):
            order += [(HEADS_PER_BLOCK * c) * GROUP_B + g, (HEADS_PER_BLOCK * c + 1) * GROUP_B + g]
    return np.array(order, dtype=np.int32)


def kernel(x, ffn1_w_in, ffn1_w_out, ffn2_w_in, ffn2_w_out, ln_g, ln_b,
           a_w_qkv, a_w_o, kv_w, b_w_q, b_sinks, b_w_o):
    b, seq, d = x.shape
    t = b * seq
    xs = x.reshape(t, d)
    slopes = _alibi_slopes(N_HEADS)
    no_sinks = jnp.zeros((N_HEADS,), F32)
    dil_patterns = tuple((dil, win // dil, float(dil)) for win, dil in DILATED_PATTERNS)
    swa_patterns = ((1, WINDOW_B - 1, 1.0),)
    n_blocks = N_HEADS // HEADS_PER_BLOCK
    head_order = _swa_head_order()
    col_order = (head_order[:, None] * HEAD_DIM + np.arange(HEAD_DIM)[None, :]).reshape(-1)
    kv_blocks = N_KV_B // HEADS_PER_BLOCK
    kv = None
    for i in range(DEPTH):
        xs = _ffn_ln(xs, ffn1_w_in[i].astype(BF16), ffn1_w_out[i].astype(BF16),
                     ln_g[i, 0], ln_b[i, 0])
        if i < N_A_LAYERS:
            qkv = _proj(xs, a_w_qkv[i].astype(BF16)).reshape(b, seq, 3 * d)
            att = _attention(
                qkv, qkv, jnp.asarray(slopes), no_sinks,
                q_block=lambda j: j, k_block=lambda j: n_blocks + j,
                v_block=lambda j: 2 * n_blocks + j,
                patterns=dil_patterns, use_sinks=False)
            w_o = a_w_o[i].astype(BF16)
        else:
            j = i - N_A_LAYERS
            q = _proj(xs, b_w_q[j][:, col_order].astype(BF16)).reshape(b, seq, d)
            att = _attention(
                q, kv, jnp.asarray(slopes[head_order]), b_sinks[j][head_order],
                q_block=lambda jb: jb, k_block=lambda jb: jb // GROUP_B,
                v_block=lambda jb: kv_blocks + jb // GROUP_B,
                patterns=swa_patterns, use_sinks=True)
            w_o = b_w_o[j][col_order, :].astype(BF16)
        xs = _oproj_ln(att.reshape(t, d), xs, w_o, ln_g[i, 1], ln_b[i, 1])
        xs = _ffn_ln(xs, ffn2_w_in[i].astype(BF16), ffn2_w_out[i].astype(BF16),
                     ln_g[i, 2], ln_b[i, 2])
        if i == N_A_LAYERS - 1:
            kv = _proj(xs, kv_w.astype(BF16)).reshape(b, seq, 2 * N_KV_B * HEAD_DIM)
    return xs.reshape(b, seq, d)
```

```python
import functools
import math

import numpy as np
import jax
import jax.numpy as jnp
from jax import lax
from jax.experimental import pallas as pl
from jax.experimental.pallas import tpu as pltpu

D_MODEL = 1024
DEPTH = 2
HEAD_DIM = 64
N_HEADS = D_MODEL // HEAD_DIM
N_KV_B = 4
GROUP_B = N_HEADS // N_KV_B
D_FF = 2816
DILATED_PATTERNS = ((128, 1), (512, 4), (2048, 16))
MAX_DILATION = max(d for _, d in DILATED_PATTERNS)
WINDOW_B = 128
BLOCK = 128
N_A_LAYERS = DEPTH // 2
ALPHA = (2.0 * DEPTH) ** 0.25
LN_EPS = 1e-5

LANES = 128
HEADS_PER_BLOCK = LANES // HEAD_DIM
MASK_VALUE = -1e30
LOG2E = math.log2(math.e)
Q_SCALE = HEAD_DIM ** -0.5 * LOG2E
VMEM_LIMIT = 56 * 1024 * 1024
TASK_UNROLL = 16
ROW_TILE = 512

BF16 = jnp.bfloat16
F32 = jnp.float32


def _layer_norm(z, g, b):
    mu = jnp.mean(z, axis=-1, keepdims=True)
    zc = z - mu
    var = jnp.mean(zc * zc, axis=-1, keepdims=True)
    return zc * lax.rsqrt(var + LN_EPS) * g + b


def _resident(shape):
    return pl.BlockSpec(shape, lambda *_: (0,) * len(shape), pipeline_mode=pl.Buffered(1))


def _row_specs(t, d, seq, strided):
    del d
    if not strided:
        return (t // ROW_TILE,), (lambda w: pl.BlockSpec((ROW_TILE, w), lambda i: (i, 0))), None
    per_class = seq // MAX_DILATION
    nb = ROW_TILE // per_class
    grid = (t // seq // nb, MAX_DILATION)
    nat = lambda w: pl.BlockSpec((nb, per_class, w), lambda i, r: (i, 0, r))
    cls = lambda w: pl.BlockSpec((nb, None, per_class, w), lambda i, r: (i, r, 0, 0))
    return grid, nat, cls


def _nat_view(x, seq):
    t, w = x.shape
    return x.reshape(t // seq, seq // MAX_DILATION, MAX_DILATION * w)


def _ffn_ln_kernel(x_ref, win_ref, wout_ref, g_ref, b_ref, o_ref):
    x = x_ref[...]
    h = jnp.dot(x.astype(BF16), win_ref[...], preferred_element_type=F32)
    gate = h[:, :D_FF]
    up = h[:, D_FF:]
    act = gate * jax.nn.sigmoid(gate) * up
    y = jnp.dot(act.astype(BF16), wout_ref[...], preferred_element_type=F32)
    z = ALPHA * x + 0.5 * y
    o_ref[...] = _layer_norm(z, g_ref[...], b_ref[...])


def _ffn_ln(x, w_in, w_out, g, b):
    t, d = x.shape
    tm = ROW_TILE
    return pl.pallas_call(
        _ffn_ln_kernel,
        grid=(t // tm,),
        in_specs=[
            pl.BlockSpec((tm, d), lambda i: (i, 0)),
            _resident(w_in.shape),
            _resident(w_out.shape),
            _resident((1, d)),
            _resident((1, d)),
        ],
        out_specs=pl.BlockSpec((tm, d), lambda i: (i, 0)),
        out_shape=jax.ShapeDtypeStruct((t, d), F32),
        compiler_params=pltpu.CompilerParams(
            dimension_semantics=("arbitrary",), vmem_limit_bytes=VMEM_LIMIT),
        name="ffn_ln",
    )(x, w_in, w_out, g.reshape(1, d), b.reshape(1, d))


def _proj_kernel(x_ref, w_ref, o_ref, *, n_scaled):
    d = x_ref.shape[-1]
    x = x_ref[...].reshape(ROW_TILE, d)
    res = jnp.dot(x.astype(BF16), w_ref[...], preferred_element_type=F32)
    if n_scaled:
        col = lax.broadcasted_iota(jnp.int32, (1, res.shape[1]), 1)
        res = res * jnp.where(col < n_scaled, Q_SCALE, 1.0)
    o_ref[...] = res.astype(o_ref.dtype).reshape(o_ref.shape)


def _proj(x, w, *, seq, out_dtype, n_scaled=0, class_major=False):
    t, d = x.shape
    n = w.shape[1]
    grid, nat, cls = _row_specs(t, d, seq, class_major)
    if class_major:
        x_in, out_spec = _nat_view(x, seq), cls(n)
        out_shape = (t // seq, MAX_DILATION, seq // MAX_DILATION, n)
    else:
        x_in, out_spec, out_shape = x, nat(n), (t, n)
    out = pl.pallas_call(
        functools.partial(_proj_kernel, n_scaled=n_scaled),
        grid=grid,
        in_specs=[nat(d), _resident(w.shape)],
        out_specs=out_spec,
        out_shape=jax.ShapeDtypeStruct(out_shape, out_dtype),
        compiler_params=pltpu.CompilerParams(
            dimension_semantics=("arbitrary",) * len(grid), vmem_limit_bytes=VMEM_LIMIT),
        name="proj",
    )(x_in, w)
    return out.reshape(t // seq, seq, n)


def _oproj_ln_kernel(a_ref, x_ref, w_ref, g_ref, b_ref, o_ref):
    d = x_ref.shape[-1]
    a = a_ref[...].reshape(ROW_TILE, a_ref.shape[-1])
    x = x_ref[...].reshape(ROW_TILE, d)
    mix = jnp.dot(a.astype(BF16), w_ref[...], preferred_element_type=F32)
    z = ALPHA * x + mix
    o_ref[...] = _layer_norm(z, g_ref[...], b_ref[...]).reshape(o_ref.shape)


def _oproj_ln(a, x, w, g, b, *, seq, class_major=False):
    t, d = x.shape
    grid, nat, cls = _row_specs(t, d, seq, class_major)
    if class_major:
        a_in = a.reshape(t // seq, MAX_DILATION, seq // MAX_DILATION, d)
        a_spec, x_in = cls(d), _nat_view(x, seq)
    else:
        a_in, a_spec, x_in = a.reshape(t, d), nat(d), x
    out = pl.pallas_call(
        _oproj_ln_kernel,
        grid=grid,
        in_specs=[a_spec, nat(d), _resident(w.shape), _resident((1, d)), _resident((1, d))],
        out_specs=nat(d),
        out_shape=jax.ShapeDtypeStruct(x_in.shape, F32),
        compiler_params=pltpu.CompilerParams(
            dimension_semantics=("arbitrary",) * len(grid), vmem_limit_bytes=VMEM_LIMIT),
        name="oproj_ln",
    )(a_in, x_in, w, g.reshape(1, d), b.reshape(1, d))
    return out.reshape(t, d)


def _attn_kernel(slope_ref, sink_ref, q_ref, k_ref, v_ref, o_ref, bias_ref, *scratch,
                 patterns, use_sinks):
    seq = q_ref.shape[0]
    blk = pl.program_id(1)
    multi = len(patterns) > 1
    if multi:
        acc_ref, lse_ref = scratch

    row = lax.broadcasted_iota(jnp.int32, (BLOCK, 2 * BLOCK), 0)
    col = lax.broadcasted_iota(jnp.int32, (BLOCK, 2 * BLOCK), 1)
    in_cur = col >= BLOCK
    lane = lax.broadcasted_iota(jnp.int32, (BLOCK, LANES), 1)
    low_half = lane < HEAD_DIM
    head_sel = [jnp.where((lane >= h * HEAD_DIM) & (lane < (h + 1) * HEAD_DIM), 1.0, 0.0).astype(BF16)
                for h in range(HEADS_PER_BLOCK)]

    for pi, (n_chunks, _, max_dist, scale) in enumerate(patterns):
        chunk_len = BLOCK // n_chunks
        sh = chunk_len.bit_length() - 1

        def pos(x, n_chunks=n_chunks, chunk_len=chunk_len, sh=sh):
            return n_chunks * (x & (chunk_len - 1)) + (x >> sh)

        q_pos = pos(row)
        k_pos = pos(col & (BLOCK - 1))
        dist = q_pos - k_pos + jnp.where(in_cur, 0, BLOCK)
        valid = (dist >= 0) & (dist <= max_dist)
        for h in range(HEADS_PER_BLOCK):
            slope = slope_ref[blk * HEADS_PER_BLOCK + h]
            bias = jnp.where(valid, (-slope * scale * LOG2E) * dist.astype(F32), MASK_VALUE)
            bias_ref[(pi * 2 + h) * 2 + 0] = bias
            bias_ref[(pi * 2 + h) * 2 + 1] = jnp.where(in_cur, bias, MASK_VALUE)
    if use_sinks:
        sink2 = jnp.where(low_half, sink_ref[blk * HEADS_PER_BLOCK] * LOG2E,
                          sink_ref[blk * HEADS_PER_BLOCK + 1] * LOG2E)

    for pi, (n_chunks, n_classes, _, _) in enumerate(patterns):
        chunk_len = BLOCK // n_chunks
        class_rows = seq // (n_chunks * n_classes)
        blocks_per_class = class_rows // chunk_len
        shift = blocks_per_class.bit_length() - 1
        assert blocks_per_class == 1 << shift and class_rows % chunk_len == 0

        def chunks(cls, n, n_chunks=n_chunks, n_classes=n_classes, chunk_len=chunk_len,
                   class_rows=class_rows):
            return [pl.ds(pl.multiple_of((n_classes * c + cls) * class_rows + n * chunk_len,
                                         chunk_len), chunk_len) for c in range(n_chunks)]

        def load(ref, slices):
            parts = [ref[s, :] for s in slices]
            return parts[0] if len(parts) == 1 else jnp.concatenate(parts, axis=0)

        def store(ref_setter, slices, val, chunk_len=chunk_len):
            for c, s in enumerate(slices):
                ref_setter(s, val[c * chunk_len:(c + 1) * chunk_len])

        def task(t, carry, pi=pi, shift=shift, blocks_per_class=blocks_per_class,
                 chunks=chunks, load=load, store=store):
            cls = lax.shift_right_logical(t, shift)
            n = lax.bitwise_and(t, blocks_per_class - 1)
            first = (n == 0).astype(jnp.int32)
            cur = chunks(cls, n)
            prev = chunks(cls, jnp.maximum(n - 1, 0))
            q = load(q_ref, cur).astype(BF16)
            kk = jnp.concatenate([load(k_ref, prev), load(k_ref, cur)], axis=0).astype(BF16)
            vv = jnp.concatenate([load(v_ref, prev), load(v_ref, cur)], axis=0).astype(BF16)
            q2 = jnp.concatenate([q * head_sel[h] for h in range(HEADS_PER_BLOCK)], axis=0)
            s2 = lax.dot_general(q2, kk, (((1,), (1,)), ((), ())),
                                 preferred_element_type=F32)
            ps, ms, dens = [], [], []
            for h in range(HEADS_PER_BLOCK):
                s = s2[h * BLOCK:(h + 1) * BLOCK] + bias_ref[(pi * 2 + h) * 2 + first]
                m = jnp.max(s, axis=-1, keepdims=True)
                p = jnp.exp2(s - m)
                dens.append(jnp.sum(p, axis=-1, keepdims=True))
                ms.append(m)
                ps.append(p.astype(BF16))
            o2 = jnp.dot(jnp.concatenate(ps, axis=0), vv, preferred_element_type=F32)
            den = jnp.where(low_half, dens[0], dens[1])
            m2 = jnp.where(low_half, ms[0], ms[1])
            if use_sinks:
                den = den + jnp.exp2(sink2 - m2)
            o = jnp.where(low_half, o2[:BLOCK], o2[BLOCK:]) / den
            if multi:
                lse = m2 + jnp.log2(den)

                def set_acc(s, val):
                    acc_ref[pi, s, :] = val

                def set_lse(s, val):
                    lse_ref[pi, s, :] = val

                store(set_acc, cur, o)
                store(set_lse, cur, lse)
            else:
                def set_out(s, val):
                    o_ref[s, :] = val.astype(o_ref.dtype)

                store(set_out, cur, o)
            return carry

        lax.fori_loop(0, seq // BLOCK, task, 0, unroll=TASK_UNROLL)

    if multi:
        chunk = 256

        def combine(i, carry):
            rows = pl.ds(pl.multiple_of(i * chunk, chunk), chunk)
            ls = [lse_ref[pi, rows, :] for pi in range(len(patterns))]
            mx = functools.reduce(jnp.maximum, ls)
            es = [jnp.exp2(l - mx) for l in ls]
            tot = functools.reduce(lambda a, b: a + b, es)
            num = functools.reduce(lambda a, b: a + b,
                                   [e * acc_ref[pi, rows, :] for pi, e in enumerate(es)])
            o_ref[rows, :] = (num / tot).astype(o_ref.dtype)
            return carry

        lax.fori_loop(0, seq // chunk, combine, 0)


def _attention(q_arr, kv_arr, slopes, sinks, *, q_block, k_block, v_block, patterns, use_sinks):
    b, seq, _ = q_arr.shape
    n_blocks = N_HEADS // HEADS_PER_BLOCK
    n_pat = len(patterns)
    scratch = [pltpu.VMEM((n_pat * 4, BLOCK, 2 * BLOCK), F32)]
    if n_pat > 1:
        scratch += [pltpu.VMEM((n_pat, seq, LANES), F32), pltpu.VMEM((n_pat, seq, LANES), F32)]
    smem = pl.BlockSpec(memory_space=pltpu.SMEM)
    return pl.pallas_call(
        functools.partial(_attn_kernel, patterns=patterns, use_sinks=use_sinks),
        grid=(b, n_blocks),
        in_specs=[
            smem, smem,
            pl.BlockSpec((None, seq, LANES), lambda i, j: (i, 0, q_block(j))),
            pl.BlockSpec((None, seq, LANES), lambda i, j: (i, 0, k_block(j))),
            pl.BlockSpec((None, seq, LANES), lambda i, j: (i, 0, v_block(j))),
        ],
        out_specs=pl.BlockSpec((None, seq, LANES), lambda i, j: (i, 0, j)),
        out_shape=jax.ShapeDtypeStruct((b, seq, D_MODEL), BF16),
        scratch_shapes=scratch,
        compiler_params=pltpu.CompilerParams(
            dimension_semantics=("arbitrary", "arbitrary"), vmem_limit_bytes=VMEM_LIMIT),
        name="attn_dilated" if n_pat > 1 else "attn_swa",
    )(slopes, sinks, q_arr, kv_arr, kv_arr)


def _alibi_slopes(n):
    return np.array([2.0 ** (-8.0 * (h + 1) / n) for h in range(n)], dtype=np.float32)


def _swa_head_order():
    order = []
    for c in range(N_KV_B // HEADS_PER_BLOCK):
        for g in range(GROUP_B):
            order += [(HEADS_PER_BLOCK * c) * GROUP_B + g, (HEADS_PER_BLOCK * c + 1) * GROUP_B + g]
    return np.array(order, dtype=np.int32)


def kernel(x, ffn1_w_in, ffn1_w_out, ffn2_w_in, ffn2_w_out, ln_g, ln_b,
           a_w_qkv, a_w_o, kv_w, b_w_q, b_sinks, b_w_o):
    b, seq, d = x.shape
    t = b * seq
    xs = x.reshape(t, d)
    slopes = _alibi_slopes(N_HEADS)
    no_sinks = jnp.zeros((N_HEADS,), F32)
    dil_patterns = tuple((MAX_DILATION // dil, dil, win // dil, float(dil))
                         for win, dil in DILATED_PATTERNS)
    swa_patterns = ((1, 1, WINDOW_B - 1, 1.0),)
    n_blocks = N_HEADS // HEADS_PER_BLOCK
    head_order = _swa_head_order()
    col_order = (head_order[:, None] * HEAD_DIM + np.arange(HEAD_DIM)[None, :]).reshape(-1)
    kv_blocks = N_KV_B // HEADS_PER_BLOCK
    kv = None
    for i in range(DEPTH):
        xs = _ffn_ln(xs, ffn1_w_in[i].astype(BF16), ffn1_w_out[i].astype(BF16),
                     ln_g[i, 0], ln_b[i, 0])
        if i < N_A_LAYERS:
            qkv = _proj(xs, a_w_qkv[i].astype(BF16), seq=seq, out_dtype=F32,
                        n_scaled=d, class_major=True)
            att = _attention(
                qkv, qkv, jnp.asarray(slopes), no_sinks,
                q_block=lambda j: j, k_block=lambda j: n_blocks + j,
                v_block=lambda j: 2 * n_blocks + j,
                patterns=dil_patterns, use_sinks=False)
            xs = _oproj_ln(att, xs, a_w_o[i].astype(BF16), ln_g[i, 1], ln_b[i, 1],
                           seq=seq, class_major=True)
        else:
            j = i - N_A_LAYERS
            q = _proj(xs, b_w_q[j][:, col_order].astype(BF16), seq=seq, out_dtype=BF16,
                      n_scaled=d)
            att = _attention(
                q, kv, jnp.asarray(slopes[head_order]), b_sinks[j][head_order],
                q_block=lambda jb: jb, k_block=lambda jb: jb // GROUP_B,
                v_block=lambda jb: kv_blocks + jb // GROUP_B,
                patterns=swa_patterns, use_sinks=True)
            xs = _oproj_ln(att, xs, b_w_o[j][col_order, :].astype(BF16), ln_g[i, 1], ln_b[i, 1],
                           seq=seq)
        xs = _ffn_ln(xs, ffn2_w_in[i].astype(BF16), ffn2_w_out[i].astype(BF16),
                     ln_g[i, 2], ln_b[i, 2])
        if i == N_A_LAYERS - 1:
            kv = _proj(xs, kv_w.astype(BF16), seq=seq, out_dtype=BF16)
    return xs.reshape(b, seq, d)
```

```python
import functools
import math

import numpy as np
import jax
import jax.numpy as jnp
from jax import lax
from jax.experimental import pallas as pl
from jax.experimental.pallas import tpu as pltpu

D_MODEL = 1024
DEPTH = 2
HEAD_DIM = 64
N_HEADS = D_MODEL // HEAD_DIM
N_KV_B = 4
GROUP_B = N_HEADS // N_KV_B
D_FF = 2816
DILATED_PATTERNS = ((128, 1), (512, 4), (2048, 16))
MAX_DILATION = max(d for _, d in DILATED_PATTERNS)
WINDOW_B = 128
BLOCK = 128
N_A_LAYERS = DEPTH // 2
ALPHA = (2.0 * DEPTH) ** 0.25
LN_EPS = 1e-5

LANES = 128
HEADS_PER_BLOCK = LANES // HEAD_DIM
MASK_VALUE = -1e30
LOG2E = math.log2(math.e)
Q_SCALE = HEAD_DIM ** -0.5 * LOG2E
VMEM_LIMIT = 56 * 1024 * 1024
TASK_UNROLL = 16
ROW_TILE = 512

BF16 = jnp.bfloat16
F32 = jnp.float32


def _layer_norm(z, g, b):
    mu = jnp.mean(z, axis=-1, keepdims=True)
    zc = z - mu
    var = jnp.mean(zc * zc, axis=-1, keepdims=True)
    return zc * lax.rsqrt(var + LN_EPS) * g + b


def _resident(shape):
    return pl.BlockSpec(shape, lambda *_: (0,) * len(shape), pipeline_mode=pl.Buffered(1))


CLASS_ROWS = ROW_TILE // MAX_DILATION
N_SLABS = D_MODEL // LANES


def _tile_spec(order, seq, w):
    if order == "nat":
        return pl.BlockSpec((ROW_TILE, w), lambda i: (i, 0))
    tiles_per_batch = seq // ROW_TILE
    return pl.BlockSpec((None, MAX_DILATION, CLASS_ROWS, w),
                        lambda i: (i // tiles_per_batch, 0, i % tiles_per_batch, 0))


def _tile_view(x, order, seq):
    t, w = x.shape
    if order == "nat":
        return x
    return x.reshape(t // seq, MAX_DILATION, seq // MAX_DILATION, w)


def _class_rows(r):
    return pl.ds(r, CLASS_ROWS, stride=MAX_DILATION)


def _block_kernel(*refs, has_attn, has_proj, proj_scale, in_order, out_order):
    refs = list(refs)
    x_ref = refs.pop(0)
    if has_attn:
        a_ref, wo_ref, g1_ref, b1_ref = refs[:4]
        del refs[:4]
    win_ref, wout_ref, g2_ref, b2_ref = refs[:4]
    del refs[:4]
    wp_ref = refs.pop(0) if has_proj else None
    o_ref = refs.pop(0)
    p_ref = refs.pop(0) if has_proj else None
    slab_ref = refs.pop(0) if in_order != out_order else None

    d = x_ref.shape[-1]
    x = x_ref[...].reshape(ROW_TILE, d)
    if has_attn:
        a = a_ref[...].reshape(ROW_TILE, a_ref.shape[-1])
        mix = jnp.dot(a.astype(BF16), wo_ref[...], preferred_element_type=F32)
        x = _layer_norm(ALPHA * x + mix, g1_ref[...], b1_ref[...])
    h = jnp.dot(x.astype(BF16), win_ref[...], preferred_element_type=F32)
    act = h[:, :D_FF] * jax.nn.sigmoid(h[:, :D_FF]) * h[:, D_FF:]
    ffn = jnp.dot(act.astype(BF16), wout_ref[...], preferred_element_type=F32)
    y = _layer_norm(ALPHA * x + 0.5 * ffn, g2_ref[...], b2_ref[...])

    if in_order == out_order:
        o_ref[...] = y.reshape(o_ref.shape)
    elif out_order == "cm":
        for j in range(N_SLABS):
            slab_ref[j] = y[:, j * LANES:(j + 1) * LANES]
        for r in range(MAX_DILATION):
            o_ref[r] = jnp.concatenate(
                [slab_ref[j, _class_rows(r), :] for j in range(N_SLABS)], axis=1)
    else:
        for r in range(MAX_DILATION):
            for j in range(N_SLABS):
                slab_ref[j, _class_rows(r), :] = y[r * CLASS_ROWS:(r + 1) * CLASS_ROWS,
                                                   j * LANES:(j + 1) * LANES]
        y = jnp.concatenate([slab_ref[j] for j in range(N_SLABS)], axis=1)
        o_ref[...] = y
    if has_proj:
        assert out_order == "nat"
        res = jnp.dot(y.astype(BF16), wp_ref[...], preferred_element_type=F32)
        if proj_scale is not None:
            res = res * proj_scale
        p_ref[...] = res.astype(p_ref.dtype)


def _block(x, w_in, w_out, g, b, *, seq, in_order, out_order, attn=None, proj=None):
    t, d = x.shape
    vec = lambda v: v.reshape(1, d)
    args, specs = [_tile_view(x, in_order, seq)], [_tile_spec(in_order, seq, d)]
    if attn is not None:
        att, w_o, g1, b1 = attn
        args += [_tile_view(att, in_order, seq), w_o, vec(g1), vec(b1)]
        specs += [_tile_spec(in_order, seq, d), _resident(w_o.shape),
                  _resident((1, d)), _resident((1, d))]
    args += [w_in, w_out, vec(g), vec(b)]
    specs += [_resident(w_in.shape), _resident(w_out.shape), _resident((1, d)), _resident((1, d))]
    out_shapes = [jax.ShapeDtypeStruct(_tile_view(x, out_order, seq).shape, F32)]
    out_specs = [_tile_spec(out_order, seq, d)]
    proj_scale = None
    if proj is not None:
        w_p, proj_scale, p_dtype = proj
        args.append(w_p)
        specs.append(_resident(w_p.shape))
        out_shapes.append(jax.ShapeDtypeStruct((t, w_p.shape[1]), p_dtype))
        out_specs.append(_tile_spec("nat", seq, w_p.shape[1]))
    scratch = []
    if in_order != out_order:
        scratch.append(pltpu.VMEM((N_SLABS, ROW_TILE, LANES), F32))
    outs = pl.pallas_call(
        functools.partial(_block_kernel, has_attn=attn is not None, has_proj=proj is not None,
                          proj_scale=proj_scale, in_order=in_order, out_order=out_order),
        grid=(t // ROW_TILE,),
        in_specs=specs,
        out_specs=out_specs,
        out_shape=out_shapes,
        scratch_shapes=scratch,
        compiler_params=pltpu.CompilerParams(
            dimension_semantics=("arbitrary",), vmem_limit_bytes=VMEM_LIMIT),
        name="block_attn" if attn is not None else "block_ffn",
    )(*args)
    y = outs[0].reshape(t, d)
    return (y, outs[1]) if proj is not None else y


def _proj_kernel(x_ref, w_ref, o_ref, *, n_scaled):
    res = jnp.dot(x_ref[...].astype(BF16), w_ref[...], preferred_element_type=F32)
    if n_scaled:
        col = lax.broadcasted_iota(jnp.int32, (1, res.shape[1]), 1)
        res = res * jnp.where(col < n_scaled, Q_SCALE, 1.0)
    o_ref[...] = res.astype(o_ref.dtype)


def _proj(x, w, *, out_dtype, n_scaled=0):
    t, d = x.shape
    n = w.shape[1]
    return pl.pallas_call(
        functools.partial(_proj_kernel, n_scaled=n_scaled),
        grid=(t // ROW_TILE,),
        in_specs=[pl.BlockSpec((ROW_TILE, d), lambda i: (i, 0)), _resident(w.shape)],
        out_specs=pl.BlockSpec((ROW_TILE, n), lambda i: (i, 0)),
        out_shape=jax.ShapeDtypeStruct((t, n), out_dtype),
        compiler_params=pltpu.CompilerParams(
            dimension_semantics=("arbitrary",), vmem_limit_bytes=VMEM_LIMIT),
        name="proj",
    )(x, w)


def _attn_kernel(slope_ref, sink_ref, q_ref, k_ref, v_ref, o_ref, bias_ref, *scratch,
                 patterns, use_sinks):
    seq = q_ref.shape[0]
    blk = pl.program_id(1)
    multi = len(patterns) > 1
    if multi:
        acc_ref, lse_ref = scratch

    row = lax.broadcasted_iota(jnp.int32, (BLOCK, 2 * BLOCK), 0)
    col = lax.broadcasted_iota(jnp.int32, (BLOCK, 2 * BLOCK), 1)
    in_cur = col >= BLOCK
    lane = lax.broadcasted_iota(jnp.int32, (BLOCK, LANES), 1)
    low_half = lane < HEAD_DIM
    head_sel = [jnp.where((lane >= h * HEAD_DIM) & (lane < (h + 1) * HEAD_DIM), 1.0, 0.0).astype(BF16)
                for h in range(HEADS_PER_BLOCK)]

    for pi, (n_chunks, _, max_dist, scale) in enumerate(patterns):
        chunk_len = BLOCK // n_chunks
        sh = chunk_len.bit_length() - 1

        def pos(x, n_chunks=n_chunks, chunk_len=chunk_len, sh=sh):
            return n_chunks * (x & (chunk_len - 1)) + (x >> sh)

        q_pos = pos(row)
        k_pos = pos(col & (BLOCK - 1))
        dist = q_pos - k_pos + jnp.where(in_cur, 0, BLOCK)
        valid = (dist >= 0) & (dist <= max_dist)
        for h in range(HEADS_PER_BLOCK):
            slope = slope_ref[blk * HEADS_PER_BLOCK + h]
            bias = jnp.where(valid, (-slope * scale * LOG2E) * dist.astype(F32), MASK_VALUE)
            bias_ref[(pi * 2 + h) * 2 + 0] = bias
            bias_ref[(pi * 2 + h) * 2 + 1] = jnp.where(in_cur, bias, MASK_VALUE)
    if use_sinks:
        sink2 = jnp.where(low_half, sink_ref[blk * HEADS_PER_BLOCK] * LOG2E,
                          sink_ref[blk * HEADS_PER_BLOCK + 1] * LOG2E)

    for pi, (n_chunks, n_classes, _, _) in enumerate(patterns):
        chunk_len = BLOCK // n_chunks
        class_rows = seq // (n_chunks * n_classes)
        blocks_per_class = class_rows // chunk_len
        shift = blocks_per_class.bit_length() - 1
        assert blocks_per_class == 1 << shift and class_rows % chunk_len == 0

        def chunks(cls, n, n_chunks=n_chunks, n_classes=n_classes, chunk_len=chunk_len,
                   class_rows=class_rows):
            return [pl.ds(pl.multiple_of((n_classes * c + cls) * class_rows + n * chunk_len,
                                         chunk_len), chunk_len) for c in range(n_chunks)]

        def load(ref, slices):
            parts = [ref[s, :] for s in slices]
            return parts[0] if len(parts) == 1 else jnp.concatenate(parts, axis=0)

        def store(ref_setter, slices, val, chunk_len=chunk_len):
            for c, s in enumerate(slices):
                ref_setter(s, val[c * chunk_len:(c + 1) * chunk_len])

        def task(t, carry, pi=pi, shift=shift, blocks_per_class=blocks_per_class,
                 chunks=chunks, load=load, store=store):
            cls = lax.shift_right_logical(t, shift)
            n = lax.bitwise_and(t, blocks_per_class - 1)
            first = (n == 0).astype(jnp.int32)
            cur = chunks(cls, n)
            prev = chunks(cls, jnp.maximum(n - 1, 0))
            q = load(q_ref, cur).astype(BF16)
            kk = jnp.concatenate([load(k_ref, prev), load(k_ref, cur)], axis=0).astype(BF16)
            vv = jnp.concatenate([load(v_ref, prev), load(v_ref, cur)], axis=0).astype(BF16)
            q2 = jnp.concatenate([q * head_sel[h] for h in range(HEADS_PER_BLOCK)], axis=0)
            s2 = lax.dot_general(q2, kk, (((1,), (1,)), ((), ())),
                                 preferred_element_type=F32)
            ps, ms, dens = [], [], []
            for h in range(HEADS_PER_BLOCK):
                s = s2[h * BLOCK:(h + 1) * BLOCK] + bias_ref[(pi * 2 + h) * 2 + first]
                m = jnp.max(s, axis=-1, keepdims=True)
                p = jnp.exp2(s - m)
                dens.append(jnp.sum(p, axis=-1, keepdims=True))
                ms.append(m)
                ps.append(p.astype(BF16))
            o2 = jnp.dot(jnp.concatenate(ps, axis=0), vv, preferred_element_type=F32)
            den = jnp.where(low_half, dens[0], dens[1])
            m2 = jnp.where(low_half, ms[0], ms[1])
            if use_sinks:
                den = den + jnp.exp2(sink2 - m2)
            o = jnp.where(low_half, o2[:BLOCK], o2[BLOCK:]) / den
            if multi:
                lse = m2 + jnp.log2(den)

                def set_acc(s, val):
                    acc_ref[pi, s, :] = val

                def set_lse(s, val):
                    lse_ref[pi, s, :] = val

                store(set_acc, cur, o)
                store(set_lse, cur, lse)
            else:
                def set_out(s, val):
                    o_ref[s, :] = val.astype(o_ref.dtype)

                store(set_out, cur, o)
            return carry

        lax.fori_loop(0, seq // BLOCK, task, 0, unroll=TASK_UNROLL)

    if multi:
        chunk = 256

        def combine(i, carry):
            rows = pl.ds(pl.multiple_of(i * chunk, chunk), chunk)
            ls = [lse_ref[pi, rows, :] for pi in range(len(patterns))]
            mx = functools.reduce(jnp.maximum, ls)
            es = [jnp.exp2(l - mx) for l in ls]
            tot = functools.reduce(lambda a, b: a + b, es)
            num = functools.reduce(lambda a, b: a + b,
                                   [e * acc_ref[pi, rows, :] for pi, e in enumerate(es)])
            o_ref[rows, :] = (num / tot).astype(o_ref.dtype)
            return carry

        lax.fori_loop(0, seq // chunk, combine, 0)


def _attention(q_arr, kv_arr, slopes, sinks, *, q_block, k_block, v_block, patterns, use_sinks):
    b, seq, _ = q_arr.shape
    n_blocks = N_HEADS // HEADS_PER_BLOCK
    n_pat = len(patterns)
    scratch = [pltpu.VMEM((n_pat * 4, BLOCK, 2 * BLOCK), F32)]
    if n_pat > 1:
        scratch += [pltpu.VMEM((n_pat, seq, LANES), F32), pltpu.VMEM((n_pat, seq, LANES), F32)]
    smem = pl.BlockSpec(memory_space=pltpu.SMEM)
    return pl.pallas_call(
        functools.partial(_attn_kernel, patterns=patterns, use_sinks=use_sinks),
        grid=(b, n_blocks),
        in_specs=[
            smem, smem,
            pl.BlockSpec((None, seq, LANES), lambda i, j: (i, 0, q_block(j))),
            pl.BlockSpec((None, seq, LANES), lambda i, j: (i, 0, k_block(j))),
            pl.BlockSpec((None, seq, LANES), lambda i, j: (i, 0, v_block(j))),
        ],
        out_specs=pl.BlockSpec((None, seq, LANES), lambda i, j: (i, 0, j)),
        out_shape=jax.ShapeDtypeStruct((b, seq, D_MODEL), BF16),
        scratch_shapes=scratch,
        compiler_params=pltpu.CompilerParams(
            dimension_semantics=("arbitrary", "arbitrary"), vmem_limit_bytes=VMEM_LIMIT),
        name="attn_dilated" if n_pat > 1 else "attn_swa",
    )(slopes, sinks, q_arr, kv_arr, kv_arr)


def _alibi_slopes(n):
    return np.array([2.0 ** (-8.0 * (h + 1) / n) for h in range(n)], dtype=np.float32)


def _swa_head_order():
    order = []
    for c in range(N_KV_B // HEADS_PER_BLOCK):
        for g in range(GROUP_B):
            order += [(HEADS_PER_BLOCK * c) * GROUP_B + g, (HEADS_PER_BLOCK * c + 1) * GROUP_B + g]
    return np.array(order, dtype=np.int32)


def kernel(x, ffn1_w_in, ffn1_w_out, ffn2_w_in, ffn2_w_out, ln_g, ln_b,
           a_w_qkv, a_w_o, kv_w, b_w_q, b_sinks, b_w_o):
    b, seq, d = x.shape
    t = b * seq
    xs = x.reshape(t, d)
    slopes = _alibi_slopes(N_HEADS)
    no_sinks = jnp.zeros((N_HEADS,), F32)
    dil_patterns = tuple((MAX_DILATION // dil, dil, win // dil, float(dil))
                         for win, dil in DILATED_PATTERNS)
    swa_patterns = ((1, 1, WINDOW_B - 1, 1.0),)
    n_blocks = N_HEADS // HEADS_PER_BLOCK
    head_order = _swa_head_order()
    col_order = (head_order[:, None] * HEAD_DIM + np.arange(HEAD_DIM)[None, :]).reshape(-1)
    kv_blocks = N_KV_B // HEADS_PER_BLOCK
    kv = None
    order = "nat"
    for i in range(DEPTH):
        dilated = i < N_A_LAYERS
        j = i - N_A_LAYERS
        w_in, w_out = ffn1_w_in[i].astype(BF16), ffn1_w_out[i].astype(BF16)
        if dilated:
            xs = _block(xs, w_in, w_out, ln_g[i, 0], ln_b[i, 0], seq=seq,
                        in_order=order, out_order="cm")
            order = "cm"
            qkv = _proj(xs, a_w_qkv[i].astype(BF16), out_dtype=F32, n_scaled=d)
            qkv = qkv.reshape(b, seq, 3 * d)
            att = _attention(
                qkv, qkv, jnp.asarray(slopes), no_sinks,
                q_block=lambda jb: jb, k_block=lambda jb: n_blocks + jb,
                v_block=lambda jb: 2 * n_blocks + jb,
                patterns=dil_patterns, use_sinks=False)
            w_o = a_w_o[i].astype(BF16)
        else:
            xs, q = _block(xs, w_in, w_out, ln_g[i, 0], ln_b[i, 0], seq=seq,
                           in_order=order, out_order="nat",
                           proj=(b_w_q[j][:, col_order].astype(BF16), Q_SCALE, BF16))
            order = "nat"
            att = _attention(
                q.reshape(b, seq, d), kv, jnp.asarray(slopes[head_order]),
                b_sinks[j][head_order],
                q_block=lambda jb: jb, k_block=lambda jb: jb // GROUP_B,
                v_block=lambda jb: kv_blocks + jb // GROUP_B,
                patterns=swa_patterns, use_sinks=True)
            w_o = b_w_o[j][col_order, :].astype(BF16)
        share_kv = i == N_A_LAYERS - 1
        out = _block(xs, ffn2_w_in[i].astype(BF16), ffn2_w_out[i].astype(BF16),
                     ln_g[i, 2], ln_b[i, 2], seq=seq, in_order=order, out_order="nat",
                     attn=(att.reshape(t, d), w_o, ln_g[i, 1], ln_b[i, 1]),
                     proj=(kv_w.astype(BF16), None, BF16) if share_kv else None)
        order = "nat"
        if share_kv:
            xs, kv = out
            kv = kv.reshape(b, seq, 2 * N_KV_B * HEAD_DIM)
        else:
            xs = out
    return xs.reshape(b, seq, d)
```

```python
import functools
import math

import numpy as np
import jax
import jax.numpy as jnp
from jax import lax
from jax.experimental import pallas as pl
from jax.experimental.pallas import tpu as pltpu

D_MODEL = 1024
DEPTH = 2
HEAD_DIM = 64
N_HEADS = D_MODEL // HEAD_DIM
N_KV_B = 4
GROUP_B = N_HEADS // N_KV_B
D_FF = 2816
DILATED_PATTERNS = ((128, 1), (512, 4), (2048, 16))
MAX_DILATION = max(d for _, d in DILATED_PATTERNS)
WINDOW_B = 128
BLOCK = 128
N_A_LAYERS = DEPTH // 2
ALPHA = (2.0 * DEPTH) ** 0.25
LN_EPS = 1e-5

LANES = 128
HEADS_PER_BLOCK = LANES // HEAD_DIM
MASK_VALUE = -1e30
LOG2E = math.log2(math.e)
Q_SCALE = HEAD_DIM ** -0.5 * LOG2E
VMEM_LIMIT = 56 * 1024 * 1024
TASK_UNROLL = 32
ROW_TILE = 512

BF16 = jnp.bfloat16
F32 = jnp.float32


def _layer_norm(z, g, b):
    mu = jnp.mean(z, axis=-1, keepdims=True)
    zc = z - mu
    var = jnp.mean(zc * zc, axis=-1, keepdims=True)
    return zc * lax.rsqrt(var + LN_EPS) * g + b


def _resident(shape):
    return pl.BlockSpec(shape, lambda *_: (0,) * len(shape), pipeline_mode=pl.Buffered(1))


CLASS_ROWS = ROW_TILE // MAX_DILATION
N_SLABS = D_MODEL // LANES
SUB_TILES = 1
SUB_ROWS = ROW_TILE // SUB_TILES
SUB_CLASS_ROWS = CLASS_ROWS // SUB_TILES


def _tile_spec(order, seq, w):
    if order == "nat":
        return pl.BlockSpec((ROW_TILE, w), lambda i: (i, 0))
    tiles_per_batch = seq // ROW_TILE
    return pl.BlockSpec((None, MAX_DILATION, CLASS_ROWS, w),
                        lambda i: (i // tiles_per_batch, 0, i % tiles_per_batch, 0))


def _tile_view(x, order, seq):
    t, w = x.shape
    if order == "nat":
        return x
    return x.reshape(t // seq, MAX_DILATION, seq // MAX_DILATION, w)


def _class_rows(r):
    return pl.ds(r, SUB_CLASS_ROWS, stride=MAX_DILATION)


def _block_kernel(*refs, has_attn, has_proj, proj_scale, in_order, out_order):
    refs = list(refs)
    x_ref = refs.pop(0)
    if has_attn:
        a_ref, wo_ref, g1_ref, b1_ref = refs[:4]
        del refs[:4]
    win_ref, wout_ref, g2_ref, b2_ref = refs[:4]
    del refs[:4]
    wp_ref = refs.pop(0) if has_proj else None
    o_ref = refs.pop(0)
    p_ref = refs.pop(0) if has_proj else None
    slab_ref = refs.pop(0) if in_order != out_order else None

    d = x_ref.shape[-1]

    def read(ref, order, s):
        if order == "nat":
            return ref[s * SUB_ROWS:(s + 1) * SUB_ROWS, :]
        v = ref[:, s * SUB_CLASS_ROWS:(s + 1) * SUB_CLASS_ROWS, :]
        return v.reshape(SUB_ROWS, v.shape[-1])

    for s in range(SUB_TILES):
        x = read(x_ref, in_order, s)
        if has_attn:
            a = read(a_ref, in_order, s)
            mix = jnp.dot(a.astype(BF16), wo_ref[...], preferred_element_type=F32)
            x = _layer_norm(ALPHA * x + mix, g1_ref[...], b1_ref[...])
        h = jnp.dot(x.astype(BF16), win_ref[...], preferred_element_type=F32)
        act = h[:, :D_FF] * jax.nn.sigmoid(h[:, :D_FF]) * h[:, D_FF:]
        ffn = jnp.dot(act.astype(BF16), wout_ref[...], preferred_element_type=F32)
        y = _layer_norm(ALPHA * x + 0.5 * ffn, g2_ref[...], b2_ref[...])

        nat_rows = slice(s * SUB_ROWS, (s + 1) * SUB_ROWS)
        cm_rows = slice(s * SUB_CLASS_ROWS, (s + 1) * SUB_CLASS_ROWS)
        if in_order == out_order == "nat":
            o_ref[nat_rows, :] = y
        elif in_order == out_order:
            o_ref[:, cm_rows, :] = y.reshape(MAX_DILATION, SUB_CLASS_ROWS, d)
        elif out_order == "cm":
            for j in range(N_SLABS):
                slab_ref[s, j] = y[:, j * LANES:(j + 1) * LANES]
            for r in range(MAX_DILATION):
                o_ref[r, cm_rows, :] = jnp.concatenate(
                    [slab_ref[s, j, _class_rows(r), :] for j in range(N_SLABS)], axis=1)
        else:
            for r in range(MAX_DILATION):
                for j in range(N_SLABS):
                    slab_ref[s, j, _class_rows(r), :] = y[
                        r * SUB_CLASS_ROWS:(r + 1) * SUB_CLASS_ROWS, j * LANES:(j + 1) * LANES]
            y = jnp.concatenate([slab_ref[s, j] for j in range(N_SLABS)], axis=1)
            o_ref[nat_rows, :] = y
        if has_proj:
            assert out_order == "nat"
            res = jnp.dot(y.astype(BF16), wp_ref[...], preferred_element_type=F32)
            if proj_scale is not None:
                res = res * proj_scale
            p_ref[nat_rows, :] = res.astype(p_ref.dtype)


def _block(x, w_in, w_out, g, b, *, seq, in_order, out_order, attn=None, proj=None):
    t, d = x.shape
    vec = lambda v: v.reshape(1, d)
    args, specs = [_tile_view(x, in_order, seq)], [_tile_spec(in_order, seq, d)]
    if attn is not None:
        att, w_o, g1, b1 = attn
        args += [_tile_view(att, in_order, seq), w_o, vec(g1), vec(b1)]
        specs += [_tile_spec(in_order, seq, d), _resident(w_o.shape),
                  _resident((1, d)), _resident((1, d))]
    args += [w_in, w_out, vec(g), vec(b)]
    specs += [_resident(w_in.shape), _resident(w_out.shape), _resident((1, d)), _resident((1, d))]
    out_shapes = [jax.ShapeDtypeStruct(_tile_view(x, out_order, seq).shape, F32)]
    out_specs = [_tile_spec(out_order, seq, d)]
    proj_scale = None
    if proj is not None:
        w_p, proj_scale, p_dtype = proj
        args.append(w_p)
        specs.append(_resident(w_p.shape))
        out_shapes.append(jax.ShapeDtypeStruct((t, w_p.shape[1]), p_dtype))
        out_specs.append(_tile_spec("nat", seq, w_p.shape[1]))
    scratch = []
    if in_order != out_order:
        scratch.append(pltpu.VMEM((SUB_TILES, N_SLABS, SUB_ROWS, LANES), F32))
    outs = pl.pallas_call(
        functools.partial(_block_kernel, has_attn=attn is not None, has_proj=proj is not None,
                          proj_scale=proj_scale, in_order=in_order, out_order=out_order),
        grid=(t // ROW_TILE,),
        in_specs=specs,
        out_specs=out_specs,
        out_shape=out_shapes,
        scratch_shapes=scratch,
        compiler_params=pltpu.CompilerParams(
            dimension_semantics=("arbitrary",), vmem_limit_bytes=VMEM_LIMIT),
        name="block_attn" if attn is not None else "block_ffn",
    )(*args)
    y = outs[0].reshape(t, d)
    return (y, outs[1]) if proj is not None else y


def _proj_kernel(x_ref, w_ref, o_ref, *, n_scaled):
    res = jnp.dot(x_ref[...].astype(BF16), w_ref[...], preferred_element_type=F32)
    if n_scaled:
        col = lax.broadcasted_iota(jnp.int32, (1, res.shape[1]), 1)
        res = res * jnp.where(col < n_scaled, Q_SCALE, 1.0)
    o_ref[...] = res.astype(o_ref.dtype)


def _proj(x, w, *, out_dtype, n_scaled=0):
    t, d = x.shape
    n = w.shape[1]
    return pl.pallas_call(
        functools.partial(_proj_kernel, n_scaled=n_scaled),
        grid=(t // ROW_TILE,),
        in_specs=[pl.BlockSpec((ROW_TILE, d), lambda i: (i, 0)), _resident(w.shape)],
        out_specs=pl.BlockSpec((ROW_TILE, n), lambda i: (i, 0)),
        out_shape=jax.ShapeDtypeStruct((t, n), out_dtype),
        compiler_params=pltpu.CompilerParams(
            dimension_semantics=("arbitrary",), vmem_limit_bytes=VMEM_LIMIT),
        name="proj",
    )(x, w)


def _attn_kernel(slope_ref, sink_ref, q_ref, k_ref, v_ref, o_ref, bias_ref, *scratch,
                 patterns, use_sinks):
    seq = q_ref.shape[0]
    blk = pl.program_id(1)
    multi = len(patterns) > 1
    if multi:
        acc_ref, max_ref, den_ref = scratch

    row = lax.broadcasted_iota(jnp.int32, (BLOCK, 2 * BLOCK), 0)
    col = lax.broadcasted_iota(jnp.int32, (BLOCK, 2 * BLOCK), 1)
    in_cur = col >= BLOCK
    lane = lax.broadcasted_iota(jnp.int32, (BLOCK, LANES), 1)
    low_half = lane < HEAD_DIM
    head_sel = [jnp.where((lane >= h * HEAD_DIM) & (lane < (h + 1) * HEAD_DIM), 1.0, 0.0).astype(BF16)
                for h in range(HEADS_PER_BLOCK)]
    sel2 = [jnp.concatenate([sel, sel], axis=0) for sel in head_sel]

    for pi, (n_chunks, _, max_dist, scale) in enumerate(patterns):
        chunk_len = BLOCK // n_chunks
        sh = chunk_len.bit_length() - 1

        def pos(x, n_chunks=n_chunks, chunk_len=chunk_len, sh=sh):
            return n_chunks * (x & (chunk_len - 1)) + (x >> sh)

        q_pos = pos(row)
        k_pos = pos(col & (BLOCK - 1))
        dist = q_pos - k_pos + jnp.where(in_cur, 0, BLOCK)
        valid = (dist >= 0) & (dist <= max_dist)
        for h in range(HEADS_PER_BLOCK):
            slope = slope_ref[blk * HEADS_PER_BLOCK + h]
            bias = jnp.where(valid, (-slope * scale * LOG2E) * dist.astype(F32), MASK_VALUE)
            bias_ref[(pi * 2 + h) * 2 + 0] = bias
            bias_ref[(pi * 2 + h) * 2 + 1] = jnp.where(in_cur, bias, MASK_VALUE)
    if use_sinks:
        sink2 = jnp.where(low_half, sink_ref[blk * HEADS_PER_BLOCK] * LOG2E,
                          sink_ref[blk * HEADS_PER_BLOCK + 1] * LOG2E)

    for pi, (n_chunks, n_classes, _, _) in enumerate(patterns):
        chunk_len = BLOCK // n_chunks
        class_rows = seq // (n_chunks * n_classes)
        blocks_per_class = class_rows // chunk_len
        shift = blocks_per_class.bit_length() - 1
        assert blocks_per_class == 1 << shift and class_rows % chunk_len == 0

        def chunks(cls, n, n_chunks=n_chunks, n_classes=n_classes, chunk_len=chunk_len,
                   class_rows=class_rows):
            return [pl.ds(pl.multiple_of((n_classes * c + cls) * class_rows + n * chunk_len,
                                         chunk_len), chunk_len) for c in range(n_chunks)]

        def load(ref, slices):
            parts = [ref[s, :] for s in slices]
            return parts[0] if len(parts) == 1 else jnp.concatenate(parts, axis=0)

        def store(ref_setter, slices, val, chunk_len=chunk_len):
            for c, s in enumerate(slices):
                ref_setter(s, val[c * chunk_len:(c + 1) * chunk_len])

        def task(t, carry, pi=pi, shift=shift, blocks_per_class=blocks_per_class,
                 chunks=chunks, load=load, store=store):
            cls = lax.shift_right_logical(t, shift)
            n = lax.bitwise_and(t, blocks_per_class - 1)
            first = (n == 0).astype(jnp.int32)
            cur = chunks(cls, n)
            prev = chunks(cls, jnp.maximum(n - 1, 0))
            q = load(q_ref, cur).astype(BF16)
            kk = jnp.concatenate([load(k_ref, prev), load(k_ref, cur)], axis=0).astype(BF16)
            vv = jnp.concatenate([load(v_ref, prev), load(v_ref, cur)], axis=0).astype(BF16)
            q2 = jnp.concatenate([q * head_sel[h] for h in range(HEADS_PER_BLOCK)], axis=0)
            s2 = lax.dot_general(q2, kk, (((1,), (1,)), ((), ())),
                                 preferred_element_type=F32)
            ps, ms = [], []
            for h in range(HEADS_PER_BLOCK):
                s = s2[h * BLOCK:(h + 1) * BLOCK] + bias_ref[(pi * 2 + h) * 2 + first]
                m = jnp.max(s, axis=-1, keepdims=True)
                ps.append(jnp.exp2(s - m).astype(BF16))
                ms.append(m)
            rhs = jnp.concatenate(
                [jnp.concatenate([vv * sel2[h], sel2[h]], axis=1) for h in range(HEADS_PER_BLOCK)],
                axis=0)
            nd = jnp.dot(jnp.concatenate(ps, axis=1), rhs, preferred_element_type=F32)
            num, den = nd[:, :LANES], nd[:, LANES:]
            m2 = jnp.where(low_half, ms[0], ms[1])
            if multi:
                def set_ref(ref):
                    def setter(s, val):
                        ref[pi, s, :] = val
                    return setter

                store(set_ref(acc_ref), cur, num)
                store(set_ref(max_ref), cur, m2)
                store(set_ref(den_ref), cur, den)
            else:
                if use_sinks:
                    den = den + jnp.exp2(sink2 - m2)

                def set_out(s, val):
                    o_ref[s, :] = val.astype(o_ref.dtype)

                store(set_out, cur, num / den)
            return carry

        lax.fori_loop(0, seq // BLOCK, task, 0, unroll=TASK_UNROLL)

    if multi:
        chunk = 256

        def combine(i, carry):
            rows = pl.ds(pl.multiple_of(i * chunk, chunk), chunk)
            mxs = [max_ref[pi, rows, :] for pi in range(len(patterns))]
            mx = functools.reduce(jnp.maximum, mxs)
            es = [jnp.exp2(m - mx) for m in mxs]
            add = lambda a, b: a + b
            num = functools.reduce(add, [e * acc_ref[pi, rows, :] for pi, e in enumerate(es)])
            tot = functools.reduce(add, [e * den_ref[pi, rows, :] for pi, e in enumerate(es)])
            o_ref[rows, :] = (num / tot).astype(o_ref.dtype)
            return carry

        lax.fori_loop(0, seq // chunk, combine, 0)


def _attention(q_arr, kv_arr, slopes, sinks, *, q_block, k_block, v_block, patterns, use_sinks):
    b, seq, _ = q_arr.shape
    n_blocks = N_HEADS // HEADS_PER_BLOCK
    n_pat = len(patterns)
    scratch = [pltpu.VMEM((n_pat * 4, BLOCK, 2 * BLOCK), F32)]
    if n_pat > 1:
        scratch += [pltpu.VMEM((n_pat, seq, LANES), F32) for _ in range(3)]
    smem = pl.BlockSpec(memory_space=pltpu.SMEM)
    return pl.pallas_call(
        functools.partial(_attn_kernel, patterns=patterns, use_sinks=use_sinks),
        grid=(b, n_blocks),
        in_specs=[
            smem, smem,
            pl.BlockSpec((None, seq, LANES), lambda i, j: (i, 0, q_block(j))),
            pl.BlockSpec((None, seq, LANES), lambda i, j: (i, 0, k_block(j))),
            pl.BlockSpec((None, seq, LANES), lambda i, j: (i, 0, v_block(j))),
        ],
        out_specs=pl.BlockSpec((None, seq, LANES), lambda i, j: (i, 0, j)),
        out_shape=jax.ShapeDtypeStruct((b, seq, D_MODEL), BF16),
        scratch_shapes=scratch,
        compiler_params=pltpu.CompilerParams(
            dimension_semantics=("arbitrary", "arbitrary"), vmem_limit_bytes=VMEM_LIMIT),
        name="attn_dilated" if n_pat > 1 else "attn_swa",
    )(slopes, sinks, q_arr, kv_arr, kv_arr)


def _alibi_slopes(n):
    return np.array([2.0 ** (-8.0 * (h + 1) / n) for h in range(n)], dtype=np.float32)


def _swa_head_order():
    order = []
    for c in range(N_KV_B // HEADS_PER_BLOCK):
        for g in range(GROUP_B):
            order += [(HEADS_PER_BLOCK * c) * GROUP_B + g, (HEADS_PER_BLOCK * c + 1) * GROUP_B + g]
    return np.array(order, dtype=np.int32)


def kernel(x, ffn1_w_in, ffn1_w_out, ffn2_w_in, ffn2_w_out, ln_g, ln_b,
           a_w_qkv, a_w_o, kv_w, b_w_q, b_sinks, b_w_o):
    b, seq, d = x.shape
    t = b * seq
    xs = x.reshape(t, d)
    slopes = _alibi_slopes(N_HEADS)
    no_sinks = jnp.zeros((N_HEADS,), F32)
    dil_patterns = tuple((MAX_DILATION // dil, dil, win // dil, float(dil))
                         for win, dil in DILATED_PATTERNS)
    swa_patterns = ((1, 1, WINDOW_B - 1, 1.0),)
    n_blocks = N_HEADS // HEADS_PER_BLOCK
    head_order = _swa_head_order()
    col_order = (head_order[:, None] * HEAD_DIM + np.arange(HEAD_DIM)[None, :]).reshape(-1)
    kv_blocks = N_KV_B // HEADS_PER_BLOCK
    kv = None
    order = "nat"
    for i in range(DEPTH):
        dilated = i < N_A_LAYERS
        j = i - N_A_LAYERS
        w_in, w_out = ffn1_w_in[i].astype(BF16), ffn1_w_out[i].astype(BF16)
        if dilated:
            xs = _block(xs, w_in, w_out, ln_g[i, 0], ln_b[i, 0], seq=seq,
                        in_order=order, out_order="cm")
            order = "cm"
            qkv = _proj(xs, a_w_qkv[i].astype(BF16), out_dtype=F32, n_scaled=d)
            qkv = qkv.reshape(b, seq, 3 * d)
            att = _attention(
                qkv, qkv, jnp.asarray(slopes), no_sinks,
                q_block=lambda jb: jb, k_block=lambda jb: n_blocks + jb,
                v_block=lambda jb: 2 * n_blocks + jb,
                patterns=dil_patterns, use_sinks=False)
            w_o = a_w_o[i].astype(BF16)
        else:
            xs, q = _block(xs, w_in, w_out, ln_g[i, 0], ln_b[i, 0], seq=seq,
                           in_order=order, out_order="nat",
                           proj=(b_w_q[j][:, col_order].astype(BF16), Q_SCALE, BF16))
            order = "nat"
            att = _attention(
                q.reshape(b, seq, d), kv, jnp.asarray(slopes[head_order]),
                b_sinks[j][head_order],
                q_block=lambda jb: jb, k_block=lambda jb: jb // GROUP_B,
                v_block=lambda jb: kv_blocks + jb // GROUP_B,
                patterns=swa_patterns, use_sinks=True)
            w_o = b_w_o[j][col_order, :].astype(BF16)
        share_kv = i == N_A_LAYERS - 1
        out = _block(xs, ffn2_w_in[i].astype(BF16), ffn2_w_out[i].astype(BF16),
                     ln_g[i, 2], ln_b[i, 2], seq=seq, in_order=order, out_order="nat",
                     attn=(att.reshape(t, d), w_o, ln_g[i, 1], ln_b[i, 1]),
                     proj=(kv_w.astype(BF16), None, BF16) if share_kv else None)
        order = "nat"
        if share_kv:
            xs, kv = out
            kv = kv.reshape(b, seq, 2 * N_KV_B * HEAD_DIM)
        else:
            xs = out
    return xs.reshape(b, seq, d)
```

```python
import functools
import math

import numpy as np
import jax
import jax.numpy as jnp
from jax import lax
from jax.experimental import pallas as pl
from jax.experimental.pallas import tpu as pltpu

D_MODEL = 1024
DEPTH = 2
HEAD_DIM = 64
N_HEADS = D_MODEL // HEAD_DIM
N_KV_B = 4
GROUP_B = N_HEADS // N_KV_B
D_FF = 2816
DILATED_PATTERNS = ((128, 1), (512, 4), (2048, 16))
MAX_DILATION = max(d for _, d in DILATED_PATTERNS)
WINDOW_B = 128
BLOCK = 128
N_A_LAYERS = DEPTH // 2
ALPHA = (2.0 * DEPTH) ** 0.25
LN_EPS = 1e-5

LANES = 128
HEADS_PER_BLOCK = LANES // HEAD_DIM
MASK_VALUE = -1e30
LOG2E = math.log2(math.e)
Q_SCALE = HEAD_DIM ** -0.5 * LOG2E
VMEM_LIMIT = 56 * 1024 * 1024
TASK_UNROLL = 32
ROW_TILE = 512

BF16 = jnp.bfloat16
F32 = jnp.float32


def _layer_norm(z, g, b):
    mu = jnp.mean(z, axis=-1, keepdims=True)
    zc = z - mu
    var = jnp.mean(zc * zc, axis=-1, keepdims=True)
    return zc * lax.rsqrt(var + LN_EPS) * g + b


def _resident(shape):
    return pl.BlockSpec(shape, lambda *_: (0,) * len(shape), pipeline_mode=pl.Buffered(1))


CLASS_ROWS = ROW_TILE // MAX_DILATION
N_SLABS = D_MODEL // LANES
W_STEPS = 16


def _tile_index(i):
    return jnp.maximum(i - W_STEPS, 0)


def _tile_spec(order, seq, w):
    if order == "nat":
        return pl.BlockSpec((ROW_TILE, w), lambda i: (_tile_index(i), 0))
    tiles_per_batch = seq // ROW_TILE

    def index(i):
        i = _tile_index(i)
        return (i // tiles_per_batch, 0, i % tiles_per_batch, 0)

    return pl.BlockSpec((None, MAX_DILATION, CLASS_ROWS, w), index)


def _tile_view(x, order, seq):
    t, w = x.shape
    if order == "nat":
        return x
    return x.reshape(t // seq, MAX_DILATION, seq // MAX_DILATION, w)


def _weight_chunk_spec(shape):
    return pl.BlockSpec((shape[0] // W_STEPS, shape[1]),
                        lambda i: (jnp.minimum(i, W_STEPS - 1), 0))


def _weight_scratch(shape):
    return pltpu.VMEM(shape, BF16)


def _load_weight_chunks(step, pairs):
    for src_ref, dst_ref in pairs:
        rows = src_ref.shape[0]
        dst_ref[pl.ds(pl.multiple_of(step * rows, rows), rows), :] = src_ref[...].astype(BF16)


def _class_rows(r):
    return pl.ds(r, CLASS_ROWS, stride=MAX_DILATION)


def _block_kernel(*refs, has_attn, has_proj, proj_scale, in_order, out_order):
    refs = list(refs)
    x_ref = refs.pop(0)
    if has_attn:
        a_ref, wo_src, g1_ref, b1_ref = refs[:4]
        del refs[:4]
    win_src, wout_src, g2_ref, b2_ref = refs[:4]
    del refs[:4]
    wp_src = refs.pop(0) if has_proj else None
    o_ref = refs.pop(0)
    p_ref = refs.pop(0) if has_proj else None
    win_ref, wout_ref = refs[:2]
    del refs[:2]
    weights = [(win_src, win_ref), (wout_src, wout_ref)]
    if has_attn:
        wo_ref = refs.pop(0)
        weights.append((wo_src, wo_ref))
    if has_proj:
        wp_ref = refs.pop(0)
        weights.append((wp_src, wp_ref))
    slab_ref = refs.pop(0) if in_order != out_order else None
    step = pl.program_id(0)

    @pl.when(step < W_STEPS)
    def _():
        _load_weight_chunks(step, weights)

    @pl.when(step >= W_STEPS)
    def _():
        d = x_ref.shape[-1]
        x = x_ref[...].reshape(ROW_TILE, d)
        if has_attn:
            a = a_ref[...].reshape(ROW_TILE, a_ref.shape[-1])
            mix = jnp.dot(a.astype(BF16), wo_ref[...], preferred_element_type=F32)
            x = _layer_norm(ALPHA * x + mix, g1_ref[...], b1_ref[...])
        h = jnp.dot(x.astype(BF16), win_ref[...], preferred_element_type=F32)
        act = h[:, :D_FF] * jax.nn.sigmoid(h[:, :D_FF]) * h[:, D_FF:]
        ffn = jnp.dot(act.astype(BF16), wout_ref[...], preferred_element_type=F32)
        y = _layer_norm(ALPHA * x + 0.5 * ffn, g2_ref[...], b2_ref[...])

        if in_order == out_order:
            o_ref[...] = y.reshape(o_ref.shape)
        elif out_order == "cm":
            for j in range(N_SLABS):
                slab_ref[j] = y[:, j * LANES:(j + 1) * LANES]
            for r in range(MAX_DILATION):
                o_ref[r] = jnp.concatenate(
                    [slab_ref[j, _class_rows(r), :] for j in range(N_SLABS)], axis=1)
        else:
            for r in range(MAX_DILATION):
                for j in range(N_SLABS):
                    slab_ref[j, _class_rows(r), :] = y[r * CLASS_ROWS:(r + 1) * CLASS_ROWS,
                                                       j * LANES:(j + 1) * LANES]
            y = jnp.concatenate([slab_ref[j] for j in range(N_SLABS)], axis=1)
            o_ref[...] = y
        if has_proj:
            assert out_order == "nat"
            res = jnp.dot(y.astype(BF16), wp_ref[...], preferred_element_type=F32)
            if proj_scale is not None:
                res = res * proj_scale
            p_ref[...] = res.astype(p_ref.dtype)


def _block(x, w_in, w_out, g, b, *, seq, in_order, out_order, attn=None, proj=None):
    t, d = x.shape
    vec = lambda v: v.reshape(1, d)
    args, specs = [_tile_view(x, in_order, seq)], [_tile_spec(in_order, seq, d)]
    scratch = [_weight_scratch(w_in.shape), _weight_scratch(w_out.shape)]
    if attn is not None:
        att, w_o, g1, b1 = attn
        args += [_tile_view(att, in_order, seq), w_o, vec(g1), vec(b1)]
        specs += [_tile_spec(in_order, seq, d), _weight_chunk_spec(w_o.shape),
                  _resident((1, d)), _resident((1, d))]
        scratch.append(_weight_scratch(w_o.shape))
    args += [w_in, w_out, vec(g), vec(b)]
    specs += [_weight_chunk_spec(w_in.shape), _weight_chunk_spec(w_out.shape),
              _resident((1, d)), _resident((1, d))]
    out_shapes = [jax.ShapeDtypeStruct(_tile_view(x, out_order, seq).shape, F32)]
    out_specs = [_tile_spec(out_order, seq, d)]
    proj_scale = None
    if proj is not None:
        w_p, proj_scale, p_dtype = proj
        args.append(w_p)
        specs.append(_weight_chunk_spec(w_p.shape))
        scratch.append(_weight_scratch(w_p.shape))
        out_shapes.append(jax.ShapeDtypeStruct((t, w_p.shape[1]), p_dtype))
        out_specs.append(_tile_spec("nat", seq, w_p.shape[1]))
    if in_order != out_order:
        scratch.append(pltpu.VMEM((N_SLABS, ROW_TILE, LANES), F32))
    outs = pl.pallas_call(
        functools.partial(_block_kernel, has_attn=attn is not None, has_proj=proj is not None,
                          proj_scale=proj_scale, in_order=in_order, out_order=out_order),
        grid=(W_STEPS + t // ROW_TILE,),
        in_specs=specs,
        out_specs=out_specs,
        out_shape=out_shapes,
        scratch_shapes=scratch,
        compiler_params=pltpu.CompilerParams(
            dimension_semantics=("arbitrary",), vmem_limit_bytes=VMEM_LIMIT),
        name="block_attn" if attn is not None else "block_ffn",
    )(*args)
    y = outs[0].reshape(t, d)
    return (y, outs[1]) if proj is not None else y


def _proj_kernel(x_ref, w_src, o_ref, w_ref, *, n_scaled):
    step = pl.program_id(0)

    @pl.when(step < W_STEPS)
    def _():
        _load_weight_chunks(step, [(w_src, w_ref)])

    @pl.when(step >= W_STEPS)
    def _():
        res = jnp.dot(x_ref[...].astype(BF16), w_ref[...], preferred_element_type=F32)
        if n_scaled:
            col = lax.broadcasted_iota(jnp.int32, (1, res.shape[1]), 1)
            res = res * jnp.where(col < n_scaled, Q_SCALE, 1.0)
        o_ref[...] = res.astype(o_ref.dtype)


def _proj(x, w, *, out_dtype, n_scaled=0):
    t, d = x.shape
    n = w.shape[1]
    return pl.pallas_call(
        functools.partial(_proj_kernel, n_scaled=n_scaled),
        grid=(W_STEPS + t // ROW_TILE,),
        in_specs=[_tile_spec("nat", 0, d), _weight_chunk_spec(w.shape)],
        out_specs=_tile_spec("nat", 0, n),
        out_shape=jax.ShapeDtypeStruct((t, n), out_dtype),
        scratch_shapes=[_weight_scratch(w.shape)],
        compiler_params=pltpu.CompilerParams(
            dimension_semantics=("arbitrary",), vmem_limit_bytes=VMEM_LIMIT),
        name="proj",
    )(x, w)


def _attn_kernel(slope_ref, sink_ref, q_ref, k_ref, v_ref, o_ref, bias_ref, *scratch,
                 patterns, use_sinks):
    seq = q_ref.shape[0]
    blk = pl.program_id(1)
    multi = len(patterns) > 1
    if multi:
        acc_ref, max_ref, den_ref = scratch

    row = lax.broadcasted_iota(jnp.int32, (BLOCK, 2 * BLOCK), 0)
    col = lax.broadcasted_iota(jnp.int32, (BLOCK, 2 * BLOCK), 1)
    in_cur = col >= BLOCK
    lane = lax.broadcasted_iota(jnp.int32, (BLOCK, LANES), 1)
    low_half = lane < HEAD_DIM
    head_sel = [jnp.where((lane >= h * HEAD_DIM) & (lane < (h + 1) * HEAD_DIM), 1.0, 0.0).astype(BF16)
                for h in range(HEADS_PER_BLOCK)]
    sel2 = [jnp.concatenate([sel, sel], axis=0) for sel in head_sel]

    for pi, (n_chunks, _, max_dist, scale) in enumerate(patterns):
        chunk_len = BLOCK // n_chunks
        sh = chunk_len.bit_length() - 1

        def pos(x, n_chunks=n_chunks, chunk_len=chunk_len, sh=sh):
            return n_chunks * (x & (chunk_len - 1)) + (x >> sh)

        q_pos = pos(row)
        k_pos = pos(col & (BLOCK - 1))
        dist = q_pos - k_pos + jnp.where(in_cur, 0, BLOCK)
        valid = (dist >= 0) & (dist <= max_dist)
        for h in range(HEADS_PER_BLOCK):
            slope = slope_ref[blk * HEADS_PER_BLOCK + h]
            bias = jnp.where(valid, (-slope * scale * LOG2E) * dist.astype(F32), MASK_VALUE)
            bias_ref[(pi * 2 + h) * 2 + 0] = bias
            bias_ref[(pi * 2 + h) * 2 + 1] = jnp.where(in_cur, bias, MASK_VALUE)
    if use_sinks:
        sink2 = jnp.where(low_half, sink_ref[blk * HEADS_PER_BLOCK] * LOG2E,
                          sink_ref[blk * HEADS_PER_BLOCK + 1] * LOG2E)

    for pi, (n_chunks, n_classes, _, _) in enumerate(patterns):
        chunk_len = BLOCK // n_chunks
        class_rows = seq // (n_chunks * n_classes)
        blocks_per_class = class_rows // chunk_len
        shift = blocks_per_class.bit_length() - 1
        assert blocks_per_class == 1 << shift and class_rows % chunk_len == 0

        def chunks(cls, n, n_chunks=n_chunks, n_classes=n_classes, chunk_len=chunk_len,
                   class_rows=class_rows):
            return [pl.ds(pl.multiple_of((n_classes * c + cls) * class_rows + n * chunk_len,
                                         chunk_len), chunk_len) for c in range(n_chunks)]

        def load(ref, slices):
            parts = [ref[s, :] for s in slices]
            return parts[0] if len(parts) == 1 else jnp.concatenate(parts, axis=0)

        def store(ref_setter, slices, val, chunk_len=chunk_len):
            for c, s in enumerate(slices):
                ref_setter(s, val[c * chunk_len:(c + 1) * chunk_len])

        def task(t, carry, pi=pi, shift=shift, blocks_per_class=blocks_per_class,
                 chunks=chunks, load=load, store=store):
            cls = lax.shift_right_logical(t, shift)
            n = lax.bitwise_and(t, blocks_per_class - 1)
            first = (n == 0).astype(jnp.int32)
            cur = chunks(cls, n)
            prev = chunks(cls, jnp.maximum(n - 1, 0))
            q = load(q_ref, cur).astype(BF16)
            kk = jnp.concatenate([load(k_ref, prev), load(k_ref, cur)], axis=0).astype(BF16)
            vv = jnp.concatenate([load(v_ref, prev), load(v_ref, cur)], axis=0).astype(BF16)
            q2 = jnp.concatenate([q * head_sel[h] for h in range(HEADS_PER_BLOCK)], axis=0)
            s2 = lax.dot_general(q2, kk, (((1,), (1,)), ((), ())),
                                 preferred_element_type=F32)
            ps, ms = [], []
            for h in range(HEADS_PER_BLOCK):
                s = s2[h * BLOCK:(h + 1) * BLOCK] + bias_ref[(pi * 2 + h) * 2 + first]
                m = jnp.max(s, axis=-1, keepdims=True)
                ps.append(jnp.exp2(s - m).astype(BF16))
                ms.append(m)
            rhs = jnp.concatenate(
                [jnp.concatenate([vv * sel2[h], sel2[h]], axis=1) for h in range(HEADS_PER_BLOCK)],
                axis=0)
            nd = jnp.dot(jnp.concatenate(ps, axis=1), rhs, preferred_element_type=F32)
            num, den = nd[:, :LANES], nd[:, LANES:]
            m2 = jnp.where(low_half, ms[0], ms[1])
            if multi:
                def set_ref(ref):
                    def setter(s, val):
                        ref[pi, s, :] = val
                    return setter

                store(set_ref(acc_ref), cur, num)
                store(set_ref(max_ref), cur, m2)
                store(set_ref(den_ref), cur, den)
            else:
                if use_sinks:
                    den = den + jnp.exp2(sink2 - m2)

                def set_out(s, val):
                    o_ref[s, :] = val.astype(o_ref.dtype)

                store(set_out, cur, num / den)
            return carry

        lax.fori_loop(0, seq // BLOCK, task, 0, unroll=TASK_UNROLL)

    if multi:
        chunk = 256

        def combine(i, carry):
            rows = pl.ds(pl.multiple_of(i * chunk, chunk), chunk)
            mxs = [max_ref[pi, rows, :] for pi in range(len(patterns))]
            mx = functools.reduce(jnp.maximum, mxs)
            es = [jnp.exp2(m - mx) for m in mxs]
            add = lambda a, b: a + b
            num = functools.reduce(add, [e * acc_ref[pi, rows, :] for pi, e in enumerate(es)])
            tot = functools.reduce(add, [e * den_ref[pi, rows, :] for pi, e in enumerate(es)])
            o_ref[rows, :] = (num / tot).astype(o_ref.dtype)
            return carry

        lax.fori_loop(0, seq // chunk, combine, 0)


def _attention(q_arr, kv_arr, slopes, sinks, *, q_block, k_block, v_block, patterns, use_sinks):
    b, seq, _ = q_arr.shape
    n_blocks = N_HEADS // HEADS_PER_BLOCK
    n_pat = len(patterns)
    scratch = [pltpu.VMEM((n_pat * 4, BLOCK, 2 * BLOCK), F32)]
    if n_pat > 1:
        scratch += [pltpu.VMEM((n_pat, seq, LANES), F32) for _ in range(3)]
    smem = pl.BlockSpec(memory_space=pltpu.SMEM)
    return pl.pallas_call(
        functools.partial(_attn_kernel, patterns=patterns, use_sinks=use_sinks),
        grid=(b, n_blocks),
        in_specs=[
            smem, smem,
            pl.BlockSpec((None, seq, LANES), lambda i, j: (i, 0, q_block(j))),
            pl.BlockSpec((None, seq, LANES), lambda i, j: (i, 0, k_block(j))),
            pl.BlockSpec((None, seq, LANES), lambda i, j: (i, 0, v_block(j))),
        ],
        out_specs=pl.BlockSpec((None, seq, LANES), lambda i, j: (i, 0, j)),
        out_shape=jax.ShapeDtypeStruct((b, seq, D_MODEL), BF16),
        scratch_shapes=scratch,
        compiler_params=pltpu.CompilerParams(
            dimension_semantics=("arbitrary", "arbitrary"), vmem_limit_bytes=VMEM_LIMIT),
        name="attn_dilated" if n_pat > 1 else "attn_swa",
    )(slopes, sinks, q_arr, kv_arr, kv_arr)


def _alibi_slopes(n):
    return np.array([2.0 ** (-8.0 * (h + 1) / n) for h in range(n)], dtype=np.float32)


def _swa_head_order():
    order = []
    for c in range(N_KV_B // HEADS_PER_BLOCK):
        for g in range(GROUP_B):
            order += [(HEADS_PER_BLOCK * c) * GROUP_B + g, (HEADS_PER_BLOCK * c + 1) * GROUP_B + g]
    return np.array(order, dtype=np.int32)


def kernel(x, ffn1_w_in, ffn1_w_out, ffn2_w_in, ffn2_w_out, ln_g, ln_b,
           a_w_qkv, a_w_o, kv_w, b_w_q, b_sinks, b_w_o):
    b, seq, d = x.shape
    t = b * seq
    xs = x.reshape(t, d)
    slopes = _alibi_slopes(N_HEADS)
    no_sinks = jnp.zeros((N_HEADS,), F32)
    dil_patterns = tuple((MAX_DILATION // dil, dil, win // dil, float(dil))
                         for win, dil in DILATED_PATTERNS)
    swa_patterns = ((1, 1, WINDOW_B - 1, 1.0),)
    n_blocks = N_HEADS // HEADS_PER_BLOCK
    head_order = _swa_head_order()
    col_order = (head_order[:, None] * HEAD_DIM + np.arange(HEAD_DIM)[None, :]).reshape(-1)
    kv_blocks = N_KV_B // HEADS_PER_BLOCK
    kv = None
    order = "nat"
    for i in range(DEPTH):
        dilated = i < N_A_LAYERS
        j = i - N_A_LAYERS
        w_in, w_out = ffn1_w_in[i], ffn1_w_out[i]
        if dilated:
            xs = _block(xs, w_in, w_out, ln_g[i, 0], ln_b[i, 0], seq=seq,
                        in_order=order, out_order="cm")
            order = "cm"
            qkv = _proj(xs, a_w_qkv[i], out_dtype=F32, n_scaled=d)
            qkv = qkv.reshape(b, seq, 3 * d)
            att = _attention(
                qkv, qkv, jnp.asarray(slopes), no_sinks,
                q_block=lambda jb: jb, k_block=lambda jb: n_blocks + jb,
                v_block=lambda jb: 2 * n_blocks + jb,
                patterns=dil_patterns, use_sinks=False)
            w_o = a_w_o[i]
        else:
            xs, q = _block(xs, w_in, w_out, ln_g[i, 0], ln_b[i, 0], seq=seq,
                           in_order=order, out_order="nat",
                           proj=(b_w_q[j][:, col_order], Q_SCALE, BF16))
            order = "nat"
            att = _attention(
                q.reshape(b, seq, d), kv, jnp.asarray(slopes[head_order]),
                b_sinks[j][head_order],
                q_block=lambda jb: jb, k_block=lambda jb: jb // GROUP_B,
                v_block=lambda jb: kv_blocks + jb // GROUP_B,
                patterns=swa_patterns, use_sinks=True)
            w_o = b_w_o[j][col_order, :]
        share_kv = i == N_A_LAYERS - 1
        out = _block(xs, ffn2_w_in[i], ffn2_w_out[i],
                     ln_g[i, 2], ln_b[i, 2], seq=seq, in_order=order, out_order="nat",
                     attn=(att.reshape(t, d), w_o, ln_g[i, 1], ln_b[i, 1]),
                     proj=(kv_w, None, BF16) if share_kv else None)
        order = "nat"
        if share_kv:
            xs, kv = out
            kv = kv.reshape(b, seq, 2 * N_KV_B * HEAD_DIM)
        else:
            xs = out
    return xs.reshape(b, seq, d)
```

```python
import functools
import math

import numpy as np
import jax
import jax.numpy as jnp
from jax import lax
from jax.experimental import pallas as pl
from jax.experimental.pallas import tpu as pltpu

D_MODEL = 1024
DEPTH = 2
HEAD_DIM = 64
N_HEADS = D_MODEL // HEAD_DIM
N_KV_B = 4
GROUP_B = N_HEADS // N_KV_B
D_FF = 2816
DILATED_PATTERNS = ((128, 1), (512, 4), (2048, 16))
MAX_DILATION = max(d for _, d in DILATED_PATTERNS)
WINDOW_B = 128
BLOCK = 128
N_A_LAYERS = DEPTH // 2
ALPHA = (2.0 * DEPTH) ** 0.25
LN_EPS = 1e-5

LANES = 128
HEADS_PER_BLOCK = LANES // HEAD_DIM
MASK_VALUE = -1e30
LOG2E = math.log2(math.e)
Q_SCALE = HEAD_DIM ** -0.5 * LOG2E
VMEM_LIMIT = 56 * 1024 * 1024
TASK_UNROLL = 32
ROW_TILE = 512

BF16 = jnp.bfloat16
F32 = jnp.float32


def _layer_norm(z, g, b):
    mu = jnp.mean(z, axis=-1, keepdims=True)
    zc = z - mu
    var = jnp.mean(zc * zc, axis=-1, keepdims=True)
    return zc * lax.rsqrt(var + LN_EPS) * g + b


def _resident(shape):
    return pl.BlockSpec(shape, lambda *_: (0,) * len(shape), pipeline_mode=pl.Buffered(1))


CLASS_ROWS = ROW_TILE // MAX_DILATION
N_SLABS = D_MODEL // LANES
W_STEPS = 16


def _tile_spec(order, seq, w, n_tiles, lag=0):
    tile = lambda i: jnp.clip(i - W_STEPS - lag, 0, n_tiles - 1)
    if order == "nat":
        return pl.BlockSpec((ROW_TILE, w), lambda i: (tile(i), 0))
    tiles_per_batch = seq // ROW_TILE

    def index(i):
        i = tile(i)
        return (i // tiles_per_batch, 0, i % tiles_per_batch, 0)

    return pl.BlockSpec((None, MAX_DILATION, CLASS_ROWS, w), index)


def _tile_view(x, order, seq):
    t, w = x.shape
    if order == "nat":
        return x
    return x.reshape(t // seq, MAX_DILATION, seq // MAX_DILATION, w)


def _weight_chunk_spec(w, layer=None):
    rows, cols = w.shape[-2] // W_STEPS, w.shape[-1]
    chunk = lambda i: jnp.minimum(i, W_STEPS - 1)
    if layer is None:
        return pl.BlockSpec((rows, cols), lambda i: (chunk(i), 0))
    return pl.BlockSpec((None, rows, cols), lambda i: (layer, chunk(i), 0))


def _weight_scratch(w):
    return pltpu.VMEM(w.shape[-2:], BF16)


def _load_weight_chunks(step, pairs):
    for src_ref, dst_ref in pairs:
        rows = src_ref.shape[0]
        dst_ref[pl.ds(pl.multiple_of(step * rows, rows), rows), :] = src_ref[...].astype(BF16)


def _class_rows(r):
    return pl.ds(r, CLASS_ROWS, stride=MAX_DILATION)


def _block_kernel(*refs, has_attn, has_proj, proj_scale, in_order, out_order):
    refs = list(refs)
    x_ref = refs.pop(0)
    if has_attn:
        a_ref, wo_src, g1_ref, b1_ref = refs[:4]
        del refs[:4]
    win_src, wout_src, g2_ref, b2_ref = refs[:4]
    del refs[:4]
    wp_src = refs.pop(0) if has_proj else None
    o_ref = refs.pop(0)
    p_ref = refs.pop(0) if has_proj else None
    win_ref, wout_ref = refs[:2]
    del refs[:2]
    weights = [(win_src, win_ref), (wout_src, wout_ref)]
    if has_attn:
        wo_ref = refs.pop(0)
        weights.append((wo_src, wo_ref))
    if has_proj:
        wp_ref = refs.pop(0)
        weights.append((wp_src, wp_ref))
    slab_ref = refs.pop(0) if in_order != out_order else None
    z_ref = refs.pop(0)
    step = pl.program_id(0)
    last = pl.num_programs(0) - 1

    @pl.when(step < W_STEPS)
    def _():
        _load_weight_chunks(step, weights)

    @pl.when(step == 0)
    def _():
        z_ref[...] = jnp.zeros_like(z_ref)

    def compute():
        d = x_ref.shape[-1]
        x = x_ref[...].reshape(ROW_TILE, d)
        if has_attn:
            a = a_ref[...].reshape(ROW_TILE, a_ref.shape[-1])
            mix = jnp.dot(a.astype(BF16), wo_ref[...], preferred_element_type=F32)
            x = _layer_norm(ALPHA * x + mix, g1_ref[...], b1_ref[...])
        h = jnp.dot(x.astype(BF16), win_ref[...], preferred_element_type=F32)
        act = h[:, :D_FF] * jax.nn.sigmoid(h[:, :D_FF]) * h[:, D_FF:]
        ffn = jnp.dot(act.astype(BF16), wout_ref[...], preferred_element_type=F32)
        z_ref[...] = ALPHA * x + 0.5 * ffn

    def finish():
        y = _layer_norm(z_ref[...], g2_ref[...], b2_ref[...])
        if in_order == out_order:
            o_ref[...] = y.reshape(o_ref.shape)
        elif out_order == "cm":
            for j in range(N_SLABS):
                slab_ref[j] = y[:, j * LANES:(j + 1) * LANES]
            for r in range(MAX_DILATION):
                o_ref[r] = jnp.concatenate(
                    [slab_ref[j, _class_rows(r), :] for j in range(N_SLABS)], axis=1)
        else:
            for r in range(MAX_DILATION):
                for j in range(N_SLABS):
                    slab_ref[j, _class_rows(r), :] = y[r * CLASS_ROWS:(r + 1) * CLASS_ROWS,
                                                       j * LANES:(j + 1) * LANES]
            y = jnp.concatenate([slab_ref[j] for j in range(N_SLABS)], axis=1)
            o_ref[...] = y
        if has_proj:
            assert out_order == "nat"
            res = jnp.dot(y.astype(BF16), wp_ref[...], preferred_element_type=F32)
            if proj_scale is not None:
                res = res * proj_scale
            p_ref[...] = res.astype(p_ref.dtype)

    @pl.when((step >= W_STEPS) & (step < last))
    def _():
        finish()
        compute()

    @pl.when(step == last)
    def _():
        finish()


def _block(x, w_in, w_out, g, b, *, seq, in_order, out_order, attn=None, proj=None):
    t, d = x.shape
    n_tiles = t // ROW_TILE
    vec = lambda v: v.reshape(1, d)
    in_spec = lambda w: _tile_spec(in_order, seq, w, n_tiles)
    out_spec = lambda order, w: _tile_spec(order, seq, w, n_tiles, lag=1)
    args, specs = [_tile_view(x, in_order, seq)], [in_spec(d)]
    scratch = [_weight_scratch(w_in[0]), _weight_scratch(w_out[0])]
    if attn is not None:
        att, w_o, g1, b1 = attn
        args += [_tile_view(att, in_order, seq), w_o[0], vec(g1), vec(b1)]
        specs += [in_spec(d), _weight_chunk_spec(*w_o), _resident((1, d)), _resident((1, d))]
        scratch.append(_weight_scratch(w_o[0]))
    args += [w_in[0], w_out[0], vec(g), vec(b)]
    specs += [_weight_chunk_spec(*w_in), _weight_chunk_spec(*w_out),
              _resident((1, d)), _resident((1, d))]
    out_shapes = [jax.ShapeDtypeStruct(_tile_view(x, out_order, seq).shape, F32)]
    out_specs = [out_spec(out_order, d)]
    proj_scale = None
    if proj is not None:
        w_p, proj_scale, p_dtype = proj
        n_p = w_p[0].shape[-1]
        args.append(w_p[0])
        specs.append(_weight_chunk_spec(*w_p))
        scratch.append(_weight_scratch(w_p[0]))
        out_shapes.append(jax.ShapeDtypeStruct((t, n_p), p_dtype))
        out_specs.append(out_spec("nat", n_p))
    if in_order != out_order:
        scratch.append(pltpu.VMEM((N_SLABS, ROW_TILE, LANES), F32))
    scratch.append(pltpu.VMEM((ROW_TILE, d), F32))
    outs = pl.pallas_call(
        functools.partial(_block_kernel, has_attn=attn is not None, has_proj=proj is not None,
                          proj_scale=proj_scale, in_order=in_order, out_order=out_order),
        grid=(W_STEPS + n_tiles + 1,),
        in_specs=specs,
        out_specs=out_specs,
        out_shape=out_shapes,
        scratch_shapes=scratch,
        compiler_params=pltpu.CompilerParams(
            dimension_semantics=("arbitrary",), vmem_limit_bytes=VMEM_LIMIT),
        name="block_attn" if attn is not None else "block_ffn",
    )(*args)
    y = outs[0].reshape(t, d)
    return (y, outs[1]) if proj is not None else y


def _proj_kernel(x_ref, w_src, o_ref, w_ref, *, n_scaled):
    step = pl.program_id(0)

    @pl.when(step < W_STEPS)
    def _():
        _load_weight_chunks(step, [(w_src, w_ref)])

    @pl.when(step >= W_STEPS)
    def _():
        res = jnp.dot(x_ref[...].astype(BF16), w_ref[...], preferred_element_type=F32)
        if n_scaled:
            col = lax.broadcasted_iota(jnp.int32, (1, res.shape[1]), 1)
            res = res * jnp.where(col < n_scaled, Q_SCALE, 1.0)
        o_ref[...] = res.astype(o_ref.dtype)


def _proj(x, w, *, out_dtype, n_scaled=0):
    t, d = x.shape
    n = w[0].shape[-1]
    n_tiles = t // ROW_TILE
    return pl.pallas_call(
        functools.partial(_proj_kernel, n_scaled=n_scaled),
        grid=(W_STEPS + n_tiles,),
        in_specs=[_tile_spec("nat", 0, d, n_tiles), _weight_chunk_spec(*w)],
        out_specs=_tile_spec("nat", 0, n, n_tiles),
        out_shape=jax.ShapeDtypeStruct((t, n), out_dtype),
        scratch_shapes=[_weight_scratch(w[0])],
        compiler_params=pltpu.CompilerParams(
            dimension_semantics=("arbitrary",), vmem_limit_bytes=VMEM_LIMIT),
        name="proj",
    )(x, w[0])


def _attn_kernel(slope_ref, sink_ref, q_ref, k_ref, v_ref, o_ref, bias_ref, *scratch,
                 patterns, use_sinks):
    seq = q_ref.shape[0]
    blk = pl.program_id(1)
    multi = len(patterns) > 1
    if multi:
        acc_ref, max_ref, den_ref = scratch

    row = lax.broadcasted_iota(jnp.int32, (BLOCK, 2 * BLOCK), 0)
    col = lax.broadcasted_iota(jnp.int32, (BLOCK, 2 * BLOCK), 1)
    in_cur = col >= BLOCK
    lane = lax.broadcasted_iota(jnp.int32, (BLOCK, LANES), 1)
    low_half = lane < HEAD_DIM
    head_sel = [jnp.where((lane >= h * HEAD_DIM) & (lane < (h + 1) * HEAD_DIM), 1.0, 0.0).astype(BF16)
                for h in range(HEADS_PER_BLOCK)]
    sel2 = [jnp.concatenate([sel, sel], axis=0) for sel in head_sel]

    for pi, (n_chunks, _, max_dist, scale) in enumerate(patterns):
        chunk_len = BLOCK // n_chunks
        sh = chunk_len.bit_length() - 1

        def pos(x, n_chunks=n_chunks, chunk_len=chunk_len, sh=sh):
            return n_chunks * (x & (chunk_len - 1)) + (x >> sh)

        q_pos = pos(row)
        k_pos = pos(col & (BLOCK - 1))
        dist = q_pos - k_pos + jnp.where(in_cur, 0, BLOCK)
        valid = (dist >= 0) & (dist <= max_dist)
        for h in range(HEADS_PER_BLOCK):
            slope = slope_ref[blk * HEADS_PER_BLOCK + h]
            bias = jnp.where(valid, (-slope * scale * LOG2E) * dist.astype(F32), MASK_VALUE)
            bias_ref[(pi * 2 + h) * 2 + 0] = bias
            bias_ref[(pi * 2 + h) * 2 + 1] = jnp.where(in_cur, bias, MASK_VALUE)
    if use_sinks:
        sink2 = jnp.where(low_half, sink_ref[blk * HEADS_PER_BLOCK] * LOG2E,
                          sink_ref[blk * HEADS_PER_BLOCK + 1] * LOG2E)

    for pi, (n_chunks, n_classes, _, _) in enumerate(patterns):
        chunk_len = BLOCK // n_chunks
        class_rows = seq // (n_chunks * n_classes)
        blocks_per_class = class_rows // chunk_len
        shift = blocks_per_class.bit_length() - 1
        assert blocks_per_class == 1 << shift and class_rows % chunk_len == 0

        def chunks(cls, n, n_chunks=n_chunks, n_classes=n_classes, chunk_len=chunk_len,
                   class_rows=class_rows):
            return [pl.ds(pl.multiple_of((n_classes * c + cls) * class_rows + n * chunk_len,
                                         chunk_len), chunk_len) for c in range(n_chunks)]

        def load(ref, slices):
            parts = [ref[s, :] for s in slices]
            return parts[0] if len(parts) == 1 else jnp.concatenate(parts, axis=0)

        def store(ref_setter, slices, val, chunk_len=chunk_len):
            for c, s in enumerate(slices):
                ref_setter(s, val[c * chunk_len:(c + 1) * chunk_len])

        def task(t, carry, pi=pi, shift=shift, blocks_per_class=blocks_per_class,
                 chunks=chunks, load=load, store=store):
            cls = lax.shift_right_logical(t, shift)
            n = lax.bitwise_and(t, blocks_per_class - 1)
            first = (n == 0).astype(jnp.int32)
            cur = chunks(cls, n)
            prev = chunks(cls, jnp.maximum(n - 1, 0))
            q = load(q_ref, cur).astype(BF16)
            kk = jnp.concatenate([load(k_ref, prev), load(k_ref, cur)], axis=0).astype(BF16)
            vv = jnp.concatenate([load(v_ref, prev), load(v_ref, cur)], axis=0).astype(BF16)
            q2 = jnp.concatenate([q * head_sel[h] for h in range(HEADS_PER_BLOCK)], axis=0)
            s2 = lax.dot_general(q2, kk, (((1,), (1,)), ((), ())),
                                 preferred_element_type=F32)
            ps, ms = [], []
            for h in range(HEADS_PER_BLOCK):
                s = s2[h * BLOCK:(h + 1) * BLOCK] + bias_ref[(pi * 2 + h) * 2 + first]
                m = jnp.max(s, axis=-1, keepdims=True)
                ps.append(jnp.exp2(s - m).astype(BF16))
                ms.append(m)
            rhs = jnp.concatenate(
                [jnp.concatenate([vv * sel2[h], sel2[h]], axis=1) for h in range(HEADS_PER_BLOCK)],
                axis=0)
            nd = jnp.dot(jnp.concatenate(ps, axis=1), rhs, preferred_element_type=F32)
            num, den = nd[:, :LANES], nd[:, LANES:]
            m2 = jnp.where(low_half, ms[0], ms[1])
            if multi:
                def set_ref(ref):
                    def setter(s, val):
                        ref[pi, s, :] = val
                    return setter

                store(set_ref(acc_ref), cur, num)
                store(set_ref(max_ref), cur, m2)
                store(set_ref(den_ref), cur, den)
            else:
                if use_sinks:
                    den = den + jnp.exp2(sink2 - m2)

                def set_out(s, val):
                    o_ref[s, :] = val.astype(o_ref.dtype)

                store(set_out, cur, num / den)
            return carry

        lax.fori_loop(0, seq // BLOCK, task, 0, unroll=TASK_UNROLL)

    if multi:
        chunk = 256

        def combine(i, carry):
            rows = pl.ds(pl.multiple_of(i * chunk, chunk), chunk)
            mxs = [max_ref[pi, rows, :] for pi in range(len(patterns))]
            mx = functools.reduce(jnp.maximum, mxs)
            es = [jnp.exp2(m - mx) for m in mxs]
            add = lambda a, b: a + b
            num = functools.reduce(add, [e * acc_ref[pi, rows, :] for pi, e in enumerate(es)])
            tot = functools.reduce(add, [e * den_ref[pi, rows, :] for pi, e in enumerate(es)])
            o_ref[rows, :] = (num / tot).astype(o_ref.dtype)
            return carry

        lax.fori_loop(0, seq // chunk, combine, 0)


def _attention(q_arr, kv_arr, slopes, sinks, *, q_block, k_block, v_block, patterns, use_sinks):
    b, seq, _ = q_arr.shape
    n_blocks = N_HEADS // HEADS_PER_BLOCK
    n_pat = len(patterns)
    scratch = [pltpu.VMEM((n_pat * 4, BLOCK, 2 * BLOCK), F32)]
    if n_pat > 1:
        scratch += [pltpu.VMEM((n_pat, seq, LANES), F32) for _ in range(3)]
    smem = pl.BlockSpec(memory_space=pltpu.SMEM)
    return pl.pallas_call(
        functools.partial(_attn_kernel, patterns=patterns, use_sinks=use_sinks),
        grid=(b, n_blocks),
        in_specs=[
            smem, smem,
            pl.BlockSpec((None, seq, LANES), lambda i, j: (i, 0, q_block(j))),
            pl.BlockSpec((None, seq, LANES), lambda i, j: (i, 0, k_block(j))),
            pl.BlockSpec((None, seq, LANES), lambda i, j: (i, 0, v_block(j))),
        ],
        out_specs=pl.BlockSpec((None, seq, LANES), lambda i, j: (i, 0, j)),
        out_shape=jax.ShapeDtypeStruct((b, seq, D_MODEL), BF16),
        scratch_shapes=scratch,
        compiler_params=pltpu.CompilerParams(
            dimension_semantics=("arbitrary", "arbitrary"), vmem_limit_bytes=VMEM_LIMIT),
        name="attn_dilated" if n_pat > 1 else "attn_swa",
    )(slopes, sinks, q_arr, kv_arr, kv_arr)


def _alibi_slopes(n):
    return np.array([2.0 ** (-8.0 * (h + 1) / n) for h in range(n)], dtype=np.float32)


def _swa_head_order():
    order = []
    for c in range(N_KV_B // HEADS_PER_BLOCK):
        for g in range(GROUP_B):
            order += [(HEADS_PER_BLOCK * c) * GROUP_B + g, (HEADS_PER_BLOCK * c + 1) * GROUP_B + g]
    return np.array(order, dtype=np.int32)


def kernel(x, ffn1_w_in, ffn1_w_out, ffn2_w_in, ffn2_w_out, ln_g, ln_b,
           a_w_qkv, a_w_o, kv_w, b_w_q, b_sinks, b_w_o):
    b, seq, d = x.shape
    t = b * seq
    xs = x.reshape(t, d)
    slopes = _alibi_slopes(N_HEADS)
    no_sinks = jnp.zeros((N_HEADS,), F32)
    dil_patterns = tuple((MAX_DILATION // dil, dil, win // dil, float(dil))
                         for win, dil in DILATED_PATTERNS)
    swa_patterns = ((1, 1, WINDOW_B - 1, 1.0),)
    n_blocks = N_HEADS // HEADS_PER_BLOCK
    head_order = _swa_head_order()
    col_order = (head_order[:, None] * HEAD_DIM + np.arange(HEAD_DIM)[None, :]).reshape(-1)
    kv_blocks = N_KV_B // HEADS_PER_BLOCK
    kv = None
    order = "nat"
    for i in range(DEPTH):
        dilated = i < N_A_LAYERS
        j = i - N_A_LAYERS
        w_in, w_out = (ffn1_w_in, i), (ffn1_w_out, i)
        if dilated:
            xs = _block(xs, w_in, w_out, ln_g[i, 0], ln_b[i, 0], seq=seq,
                        in_order=order, out_order="cm")
            order = "cm"
            qkv = _proj(xs, (a_w_qkv, i), out_dtype=F32, n_scaled=d)
            qkv = qkv.reshape(b, seq, 3 * d)
            att = _attention(
                qkv, qkv, jnp.asarray(slopes), no_sinks,
                q_block=lambda jb: jb, k_block=lambda jb: n_blocks + jb,
                v_block=lambda jb: 2 * n_blocks + jb,
                patterns=dil_patterns, use_sinks=False)
            w_o = (a_w_o, i)
        else:
            xs, q = _block(xs, w_in, w_out, ln_g[i, 0], ln_b[i, 0], seq=seq,
                           in_order=order, out_order="nat",
                           proj=((b_w_q[j][:, col_order], None), Q_SCALE, BF16))
            order = "nat"
            att = _attention(
                q.reshape(b, seq, d), kv, jnp.asarray(slopes[head_order]),
                b_sinks[j][head_order],
                q_block=lambda jb: jb, k_block=lambda jb: jb // GROUP_B,
                v_block=lambda jb: kv_blocks + jb // GROUP_B,
                patterns=swa_patterns, use_sinks=True)
            w_o = (b_w_o[j][col_order, :], None)
        share_kv = i == N_A_LAYERS - 1
        out = _block(xs, (ffn2_w_in, i), (ffn2_w_out, i),
                     ln_g[i, 2], ln_b[i, 2], seq=seq, in_order=order, out_order="nat",
                     attn=(att.reshape(t, d), w_o, ln_g[i, 1], ln_b[i, 1]),
                     proj=((kv_w, None), None, BF16) if share_kv else None)
        order = "nat"
        if share_kv:
            xs, kv = out
            kv = kv.reshape(b, seq, 2 * N_KV_B * HEAD_DIM)
        else:
            xs = out
    return xs.reshape(b, seq, d)
```

```python
import functools
import math

import numpy as np
import jax
import jax.numpy as jnp
from jax import lax
from jax.experimental import pallas as pl
from jax.experimental.pallas import tpu as pltpu

D_MODEL = 1024
DEPTH = 2
HEAD_DIM = 64
N_HEADS = D_MODEL // HEAD_DIM
N_KV_B = 4
GROUP_B = N_HEADS // N_KV_B
D_FF = 2816
DILATED_PATTERNS = ((128, 1), (512, 4), (2048, 16))
MAX_DILATION = max(d for _, d in DILATED_PATTERNS)
WINDOW_B = 128
BLOCK = 128
N_A_LAYERS = DEPTH // 2
ALPHA = (2.0 * DEPTH) ** 0.25
LN_EPS = 1e-5

LANES = 128
HEADS_PER_BLOCK = LANES // HEAD_DIM
MASK_VALUE = -1e30
LOG2E = math.log2(math.e)
Q_SCALE = HEAD_DIM ** -0.5 * LOG2E
VMEM_LIMIT = 56 * 1024 * 1024
ROW_TILE = 512

BF16 = jnp.bfloat16
F32 = jnp.float32


def _layer_norm(z, g, b):
    mu = jnp.mean(z, axis=-1, keepdims=True)
    zc = z - mu
    var = jnp.mean(zc * zc, axis=-1, keepdims=True)
    return zc * lax.rsqrt(var + LN_EPS) * g + b


def _resident(shape):
    return pl.BlockSpec(shape, lambda *_: (0,) * len(shape), pipeline_mode=pl.Buffered(1))


CLASS_ROWS = ROW_TILE // MAX_DILATION
N_SLABS = D_MODEL // LANES
W_STEPS = 16


def _tile_spec(order, seq, w, n_tiles, lag=0):
    tile = lambda i: jnp.clip(i - W_STEPS - lag, 0, n_tiles - 1)
    if order == "nat":
        return pl.BlockSpec((ROW_TILE, w), lambda i: (tile(i), 0))
    tiles_per_batch = seq // ROW_TILE

    def index(i):
        i = tile(i)
        return (i // tiles_per_batch, 0, i % tiles_per_batch, 0)

    return pl.BlockSpec((None, MAX_DILATION, CLASS_ROWS, w), index)


def _tile_view(x, order, seq):
    t, w = x.shape
    if order == "nat":
        return x
    return x.reshape(t // seq, MAX_DILATION, seq // MAX_DILATION, w)


def _weight_chunk_spec(w, layer=None):
    rows, cols = w.shape[-2] // W_STEPS, w.shape[-1]
    chunk = lambda i: jnp.minimum(i, W_STEPS - 1)
    if layer is None:
        return pl.BlockSpec((rows, cols), lambda i: (chunk(i), 0))
    return pl.BlockSpec((None, rows, cols), lambda i: (layer, chunk(i), 0))


def _weight_scratch(w):
    return pltpu.VMEM(w.shape[-2:], BF16)


def _load_weight_chunks(step, pairs):
    for src_ref, dst_ref in pairs:
        rows = src_ref.shape[0]
        dst_ref[pl.ds(pl.multiple_of(step * rows, rows), rows), :] = src_ref[...].astype(BF16)


def _class_rows(r):
    return pl.ds(r, CLASS_ROWS, stride=MAX_DILATION)


def _block_kernel(*refs, has_attn, has_proj, proj_scale, in_order, out_order):
    refs = list(refs)
    x_ref = refs.pop(0)
    if has_attn:
        a_ref, wo_src, g1_ref, b1_ref = refs[:4]
        del refs[:4]
    win_src, wout_src, g2_ref, b2_ref = refs[:4]
    del refs[:4]
    wp_src = refs.pop(0) if has_proj else None
    o_ref = refs.pop(0)
    p_ref = refs.pop(0) if has_proj else None
    win_ref, wout_ref = refs[:2]
    del refs[:2]
    weights = [(win_src, win_ref), (wout_src, wout_ref)]
    if has_attn:
        wo_ref = refs.pop(0)
        weights.append((wo_src, wo_ref))
    if has_proj:
        wp_ref = refs.pop(0)
        weights.append((wp_src, wp_ref))
    slab_ref = refs.pop(0) if in_order != out_order else None
    z_ref = refs.pop(0)
    step = pl.program_id(0)
    last = pl.num_programs(0) - 1

    @pl.when(step < W_STEPS)
    def _():
        _load_weight_chunks(step, weights)

    @pl.when(step == 0)
    def _():
        z_ref[...] = jnp.zeros_like(z_ref)

    def compute():
        d = x_ref.shape[-1]
        x = x_ref[...].reshape(ROW_TILE, d)
        if has_attn:
            a = a_ref[...].reshape(ROW_TILE, a_ref.shape[-1])
            mix = jnp.dot(a.astype(BF16), wo_ref[...], preferred_element_type=F32)
            x = _layer_norm(ALPHA * x + mix, g1_ref[...], b1_ref[...])
        h = jnp.dot(x.astype(BF16), win_ref[...], preferred_element_type=F32)
        act = h[:, :D_FF] * jax.nn.sigmoid(h[:, :D_FF]) * h[:, D_FF:]
        ffn = jnp.dot(act.astype(BF16), wout_ref[...], preferred_element_type=F32)
        z_ref[...] = ALPHA * x + 0.5 * ffn

    def finish():
        y = _layer_norm(z_ref[...], g2_ref[...], b2_ref[...])
        if in_order == out_order:
            o_ref[...] = y.reshape(o_ref.shape)
        elif out_order == "cm":
            for j in range(N_SLABS):
                slab_ref[j] = y[:, j * LANES:(j + 1) * LANES]
            for r in range(MAX_DILATION):
                o_ref[r] = jnp.concatenate(
                    [slab_ref[j, _class_rows(r), :] for j in range(N_SLABS)], axis=1)
        else:
            for r in range(MAX_DILATION):
                for j in range(N_SLABS):
                    slab_ref[j, _class_rows(r), :] = y[r * CLASS_ROWS:(r + 1) * CLASS_ROWS,
                                                       j * LANES:(j + 1) * LANES]
            y = jnp.concatenate([slab_ref[j] for j in range(N_SLABS)], axis=1)
            o_ref[...] = y
        if has_proj:
            assert out_order == "nat"
            res = jnp.dot(y.astype(BF16), wp_ref[...], preferred_element_type=F32)
            if proj_scale is not None:
                res = res * proj_scale
            p_ref[...] = res.astype(p_ref.dtype)

    @pl.when((step >= W_STEPS) & (step < last))
    def _():
        finish()
        compute()

    @pl.when(step == last)
    def _():
        finish()


def _block(x, w_in, w_out, g, b, *, seq, in_order, out_order, attn=None, proj=None):
    t, d = x.shape
    n_tiles = t // ROW_TILE
    vec = lambda v: v.reshape(1, d)
    in_spec = lambda w: _tile_spec(in_order, seq, w, n_tiles)
    out_spec = lambda order, w: _tile_spec(order, seq, w, n_tiles, lag=1)
    args, specs = [_tile_view(x, in_order, seq)], [in_spec(d)]
    scratch = [_weight_scratch(w_in[0]), _weight_scratch(w_out[0])]
    if attn is not None:
        att, w_o, g1, b1 = attn
        args += [_tile_view(att, in_order, seq), w_o[0], vec(g1), vec(b1)]
        specs += [in_spec(d), _weight_chunk_spec(*w_o), _resident((1, d)), _resident((1, d))]
        scratch.append(_weight_scratch(w_o[0]))
    args += [w_in[0], w_out[0], vec(g), vec(b)]
    specs += [_weight_chunk_spec(*w_in), _weight_chunk_spec(*w_out),
              _resident((1, d)), _resident((1, d))]
    out_shapes = [jax.ShapeDtypeStruct(_tile_view(x, out_order, seq).shape, F32)]
    out_specs = [out_spec(out_order, d)]
    proj_scale = None
    if proj is not None:
        w_p, proj_scale, p_dtype = proj
        n_p = w_p[0].shape[-1]
        args.append(w_p[0])
        specs.append(_weight_chunk_spec(*w_p))
        scratch.append(_weight_scratch(w_p[0]))
        out_shapes.append(jax.ShapeDtypeStruct((t, n_p), p_dtype))
        out_specs.append(out_spec("nat", n_p))
    if in_order != out_order:
        scratch.append(pltpu.VMEM((N_SLABS, ROW_TILE, LANES), F32))
    scratch.append(pltpu.VMEM((ROW_TILE, d), F32))
    outs = pl.pallas_call(
        functools.partial(_block_kernel, has_attn=attn is not None, has_proj=proj is not None,
                          proj_scale=proj_scale, in_order=in_order, out_order=out_order),
        grid=(W_STEPS + n_tiles + 1,),
        in_specs=specs,
        out_specs=out_specs,
        out_shape=out_shapes,
        scratch_shapes=scratch,
        compiler_params=pltpu.CompilerParams(
            dimension_semantics=("arbitrary",), vmem_limit_bytes=VMEM_LIMIT),
        name="block_attn" if attn is not None else "block_ffn",
    )(*args)
    y = outs[0].reshape(t, d)
    return (y, outs[1]) if proj is not None else y


def _proj_kernel(x_ref, w_src, o_ref, w_ref, *, n_scaled):
    step = pl.program_id(0)

    @pl.when(step < W_STEPS)
    def _():
        _load_weight_chunks(step, [(w_src, w_ref)])

    @pl.when(step >= W_STEPS)
    def _():
        res = jnp.dot(x_ref[...].astype(BF16), w_ref[...], preferred_element_type=F32)
        if n_scaled:
            col = lax.broadcasted_iota(jnp.int32, (1, res.shape[1]), 1)
            res = res * jnp.where(col < n_scaled, Q_SCALE, 1.0)
        o_ref[...] = res.astype(o_ref.dtype)


def _proj(x, w, *, out_dtype, n_scaled=0):
    t, d = x.shape
    n = w[0].shape[-1]
    n_tiles = t // ROW_TILE
    return pl.pallas_call(
        functools.partial(_proj_kernel, n_scaled=n_scaled),
        grid=(W_STEPS + n_tiles,),
        in_specs=[_tile_spec("nat", 0, d, n_tiles), _weight_chunk_spec(*w)],
        out_specs=_tile_spec("nat", 0, n, n_tiles),
        out_shape=jax.ShapeDtypeStruct((t, n), out_dtype),
        scratch_shapes=[_weight_scratch(w[0])],
        compiler_params=pltpu.CompilerParams(
            dimension_semantics=("arbitrary",), vmem_limit_bytes=VMEM_LIMIT),
        name="proj",
    )(x, w[0])


def _attn_kernel(slope_ref, sink_ref, q_ref, k_ref, v_ref, o_ref, bias_ref, *scratch,
                 patterns, use_sinks):
    seq = q_ref.shape[0]
    blk = pl.program_id(0)
    multi = len(patterns) > 1
    if multi:
        acc_ref, max_ref, den_ref = scratch

    lane = lax.broadcasted_iota(jnp.int32, (BLOCK, LANES), 1)
    low_half = lane < HEAD_DIM
    head_sel = [jnp.where((lane >= h * HEAD_DIM) & (lane < (h + 1) * HEAD_DIM), 1.0, 0.0).astype(BF16)
                for h in range(HEADS_PER_BLOCK)]
    sel2 = [jnp.concatenate([sel, sel], axis=0) for sel in head_sel]

    @pl.when(pl.program_id(1) == 0)
    def _():
        row = lax.broadcasted_iota(jnp.int32, (BLOCK, 2 * BLOCK), 0)
        col = lax.broadcasted_iota(jnp.int32, (BLOCK, 2 * BLOCK), 1)
        in_cur = col >= BLOCK
        for pi, (n_chunks, _, max_dist, scale) in enumerate(patterns):
            chunk_len = BLOCK // n_chunks
            sh = chunk_len.bit_length() - 1
            pos = lambda x: n_chunks * (x & (chunk_len - 1)) + (x >> sh)
            dist = pos(row) - pos(col & (BLOCK - 1)) + jnp.where(in_cur, 0, BLOCK)
            valid = (dist >= 0) & (dist <= max_dist)
            for h in range(HEADS_PER_BLOCK):
                slope = slope_ref[blk * HEADS_PER_BLOCK + h]
                bias = jnp.where(valid, (-slope * scale * LOG2E) * dist.astype(F32), MASK_VALUE)
                bias_ref[(pi * 2 + h) * 2 + 0] = bias
                bias_ref[(pi * 2 + h) * 2 + 1] = jnp.where(in_cur, bias, MASK_VALUE)

    if use_sinks:
        sink2 = jnp.where(low_half, sink_ref[blk * HEADS_PER_BLOCK] * LOG2E,
                          sink_ref[blk * HEADS_PER_BLOCK + 1] * LOG2E)

    def block_task(pi, cls, n):
        n_chunks, n_classes, _, _ = patterns[pi]
        chunk_len = BLOCK // n_chunks
        class_rows = seq // (n_chunks * n_classes)
        static = isinstance(n, int)

        def chunks(n):
            starts = [(n_classes * c + cls) * class_rows + n * chunk_len for c in range(n_chunks)]
            return [pl.ds(s if static else pl.multiple_of(s, chunk_len), chunk_len)
                    for s in starts]

        def load(ref, slices):
            parts = [ref[s, :] for s in slices]
            return parts[0] if len(parts) == 1 else jnp.concatenate(parts, axis=0)

        def store(ref, slices, val):
            for c, s in enumerate(slices):
                part = val[c * chunk_len:(c + 1) * chunk_len]
                if multi:
                    ref[pi, s, :] = part
                else:
                    ref[s, :] = part.astype(ref.dtype)

        if static:
            first, prev_n = int(n == 0), max(n - 1, 0)
        else:
            first, prev_n = (n == 0).astype(jnp.int32), jnp.maximum(n - 1, 0)
        cur = chunks(n)
        prev = chunks(prev_n)
        q = load(q_ref, cur).astype(BF16)
        kk = jnp.concatenate([load(k_ref, prev), load(k_ref, cur)], axis=0).astype(BF16)
        vv = jnp.concatenate([load(v_ref, prev), load(v_ref, cur)], axis=0).astype(BF16)
        q2 = jnp.concatenate([q * head_sel[h] for h in range(HEADS_PER_BLOCK)], axis=0)
        s2 = lax.dot_general(q2, kk, (((1,), (1,)), ((), ())),
                             preferred_element_type=F32)
        ps, ms = [], []
        for h in range(HEADS_PER_BLOCK):
            s = s2[h * BLOCK:(h + 1) * BLOCK] + bias_ref[(pi * 2 + h) * 2 + first]
            m = jnp.max(s, axis=-1, keepdims=True)
            ps.append(jnp.exp2(s - m).astype(BF16))
            ms.append(m)
        rhs = jnp.concatenate(
            [jnp.concatenate([vv * sel2[h], sel2[h]], axis=1) for h in range(HEADS_PER_BLOCK)],
            axis=0)
        nd = jnp.dot(jnp.concatenate(ps, axis=1), rhs, preferred_element_type=F32)
        num, den = nd[:, :LANES], nd[:, LANES:]
        m2 = jnp.where(low_half, ms[0], ms[1])
        if multi:
            store(acc_ref, cur, num)
            store(max_ref, cur, m2)
            store(den_ref, cur, den)
        else:
            if use_sinks:
                den = den + jnp.exp2(sink2 - m2)
            store(o_ref, cur, num / den)

    def combine(row0, n_rows):
        rows = pl.ds(row0, n_rows)
        mxs = [max_ref[pi, rows, :] for pi in range(len(patterns))]
        mx = functools.reduce(jnp.maximum, mxs)
        es = [jnp.exp2(m - mx) for m in mxs]
        add = lambda a, b: a + b
        num = functools.reduce(add, [e * acc_ref[pi, rows, :] for pi, e in enumerate(es)])
        tot = functools.reduce(add, [e * den_ref[pi, rows, :] for pi, e in enumerate(es)])
        o_ref[rows, :] = (num / tot).astype(o_ref.dtype)

    for pi, (n_chunks, n_classes, _, _) in enumerate(patterns):
        class_rows = seq // (n_chunks * n_classes)
        blocks_per_class = class_rows // (BLOCK // n_chunks)
        shift = blocks_per_class.bit_length() - 1
        assert blocks_per_class == 1 << shift and n_classes * blocks_per_class == seq // BLOCK
        if multi and pi == len(patterns) - 1:
            assert n_chunks == 1
            for cls in range(n_classes):
                for n in range(blocks_per_class):
                    block_task(pi, cls, n)
                combine(cls * class_rows, class_rows)
        else:
            def task(t, carry, pi=pi, shift=shift, blocks_per_class=blocks_per_class):
                block_task(pi, lax.shift_right_logical(t, shift),
                           lax.bitwise_and(t, blocks_per_class - 1))
                return carry

            lax.fori_loop(0, seq // BLOCK, task, 0, unroll=True)


def _attention(q_arr, kv_arr, slopes, sinks, *, q_block, k_block, v_block, patterns, use_sinks):
    b, seq, _ = q_arr.shape
    n_blocks = N_HEADS // HEADS_PER_BLOCK
    n_pat = len(patterns)
    scratch = [pltpu.VMEM((n_pat * 4, BLOCK, 2 * BLOCK), F32)]
    if n_pat > 1:
        scratch += [pltpu.VMEM((n_pat, seq, LANES), F32) for _ in range(3)]
    smem = pl.BlockSpec(memory_space=pltpu.SMEM)
    return pl.pallas_call(
        functools.partial(_attn_kernel, patterns=patterns, use_sinks=use_sinks),
        grid=(n_blocks, b),
        in_specs=[
            smem, smem,
            pl.BlockSpec((None, seq, LANES), lambda j, i: (i, 0, q_block(j))),
            pl.BlockSpec((None, seq, LANES), lambda j, i: (i, 0, k_block(j))),
            pl.BlockSpec((None, seq, LANES), lambda j, i: (i, 0, v_block(j))),
        ],
        out_specs=pl.BlockSpec((None, seq, LANES), lambda j, i: (i, 0, j)),
        out_shape=jax.ShapeDtypeStruct((b, seq, D_MODEL), BF16),
        scratch_shapes=scratch,
        compiler_params=pltpu.CompilerParams(
            dimension_semantics=("arbitrary", "arbitrary"), vmem_limit_bytes=VMEM_LIMIT),
        name="attn_dilated" if n_pat > 1 else "attn_swa",
    )(slopes, sinks, q_arr, kv_arr, kv_arr)


def _alibi_slopes(n):
    return np.array([2.0 ** (-8.0 * (h + 1) / n) for h in range(n)], dtype=np.float32)


def _swa_head_order():
    order = []
    for c in range(N_KV_B // HEADS_PER_BLOCK):
        for g in range(GROUP_B):
            order += [(HEADS_PER_BLOCK * c) * GROUP_B + g, (HEADS_PER_BLOCK * c + 1) * GROUP_B + g]
    return np.array(order, dtype=np.int32)


def kernel(x, ffn1_w_in, ffn1_w_out, ffn2_w_in, ffn2_w_out, ln_g, ln_b,
           a_w_qkv, a_w_o, kv_w, b_w_q, b_sinks, b_w_o):
    b, seq, d = x.shape
    t = b * seq
    xs = x.reshape(t, d)
    slopes = _alibi_slopes(N_HEADS)
    no_sinks = jnp.zeros((N_HEADS,), F32)
    dil_patterns = tuple((MAX_DILATION // dil, dil, win // dil, float(dil))
                         for win, dil in DILATED_PATTERNS)
    swa_patterns = ((1, 1, WINDOW_B - 1, 1.0),)
    n_blocks = N_HEADS // HEADS_PER_BLOCK
    head_order = _swa_head_order()
    col_order = (head_order[:, None] * HEAD_DIM + np.arange(HEAD_DIM)[None, :]).reshape(-1)
    kv_blocks = N_KV_B // HEADS_PER_BLOCK
    kv = None
    order = "nat"
    for i in range(DEPTH):
        dilated = i < N_A_LAYERS
        j = i - N_A_LAYERS
        w_in, w_out = (ffn1_w_in, i), (ffn1_w_out, i)
        if dilated:
            xs = _block(xs, w_in, w_out, ln_g[i, 0], ln_b[i, 0], seq=seq,
                        in_order=order, out_order="cm")
            order = "cm"
            qkv = _proj(xs, (a_w_qkv, i), out_dtype=F32, n_scaled=d)
            qkv = qkv.reshape(b, seq, 3 * d)
            att = _attention(
                qkv, qkv, jnp.asarray(slopes), no_sinks,
                q_block=lambda jb: jb, k_block=lambda jb: n_blocks + jb,
                v_block=lambda jb: 2 * n_blocks + jb,
                patterns=dil_patterns, use_sinks=False)
            w_o = (a_w_o, i)
        else:
            xs, q = _block(xs, w_in, w_out, ln_g[i, 0], ln_b[i, 0], seq=seq,
                           in_order=order, out_order="nat",
                           proj=((b_w_q[j][:, col_order], None), Q_SCALE, BF16))
            order = "nat"
            att = _attention(
                q.reshape(b, seq, d), kv, jnp.asarray(slopes[head_order]),
                b_sinks[j][head_order],
                q_block=lambda jb: jb, k_block=lambda jb: jb // GROUP_B,
                v_block=lambda jb: kv_blocks + jb // GROUP_B,
                patterns=swa_patterns, use_sinks=True)
            w_o = (b_w_o[j][col_order, :], None)
        share_kv = i == N_A_LAYERS - 1
        out = _block(xs, (ffn2_w_in, i), (ffn2_w_out, i),
                     ln_g[i, 2], ln_b[i, 2], seq=seq, in_order=order, out_order="nat",
                     attn=(att.reshape(t, d), w_o, ln_g[i, 1], ln_b[i, 1]),
                     proj=((kv_w, None), None, BF16) if share_kv else None)
        order = "nat"
        if share_kv:
            xs, kv = out
            kv = kv.reshape(b, seq, 2 * N_KV_B * HEAD_DIM)
        else:
            xs = out
    return xs.reshape(b, seq, d)
```

```python
import functools
import math

import numpy as np
import jax
import jax.numpy as jnp
from jax import lax
from jax.experimental import pallas as pl
from jax.experimental.pallas import tpu as pltpu

D_MODEL = 1024
DEPTH = 2
HEAD_DIM = 64
N_HEADS = D_MODEL // HEAD_DIM
N_KV_B = 4
GROUP_B = N_HEADS // N_KV_B
D_FF = 2816
DILATED_PATTERNS = ((128, 1), (512, 4), (2048, 16))
MAX_DILATION = max(d for _, d in DILATED_PATTERNS)
WINDOW_B = 128
BLOCK = 128
N_A_LAYERS = DEPTH // 2
ALPHA = (2.0 * DEPTH) ** 0.25
LN_EPS = 1e-5

LANES = 128
HEADS_PER_BLOCK = LANES // HEAD_DIM
MASK_VALUE = -1e30
LOG2E = math.log2(math.e)
Q_SCALE = HEAD_DIM ** -0.5 * LOG2E
VMEM_LIMIT = 56 * 1024 * 1024
ROW_TILE = 512

BF16 = jnp.bfloat16
F32 = jnp.float32


def _layer_norm(z, g, b):
    mu = jnp.mean(z, axis=-1, keepdims=True)
    zc = z - mu
    var = jnp.mean(zc * zc, axis=-1, keepdims=True)
    return zc * lax.rsqrt(var + LN_EPS) * g + b


def _resident(shape):
    return pl.BlockSpec(shape, lambda *_: (0,) * len(shape), pipeline_mode=pl.Buffered(1))


CLASS_ROWS = ROW_TILE // MAX_DILATION
N_SLABS = D_MODEL // LANES
W_STEPS = 16
MIX_ROWS = 128


def _tile_spec(order, seq, w, n_tiles, lag=0):
    tile = lambda i: jnp.clip(i - W_STEPS - lag, 0, n_tiles - 1)
    if order == "nat":
        return pl.BlockSpec((ROW_TILE, w), lambda i: (tile(i), 0))
    tiles_per_batch = seq // ROW_TILE

    def index(i):
        i = tile(i)
        return (i // tiles_per_batch, 0, i % tiles_per_batch, 0)

    return pl.BlockSpec((None, MAX_DILATION, CLASS_ROWS, w), index)


def _tile_view(x, order, seq):
    t, w = x.shape
    if order == "nat":
        return x
    return x.reshape(t // seq, MAX_DILATION, seq // MAX_DILATION, w)


def _weight_chunk_spec(w, layer=None):
    rows, cols = w.shape[-2] // W_STEPS, w.shape[-1]
    chunk = lambda i: jnp.minimum(i, W_STEPS - 1)
    if layer is None:
        return pl.BlockSpec((rows, cols), lambda i: (chunk(i), 0))
    return pl.BlockSpec((None, rows, cols), lambda i: (layer, chunk(i), 0))


def _weight_scratch(w):
    return pltpu.VMEM(w.shape[-2:], BF16)


def _load_weight_chunks(step, pairs):
    for src_ref, dst_ref in pairs:
        rows = src_ref.shape[0]
        dst_ref[pl.ds(pl.multiple_of(step * rows, rows), rows), :] = src_ref[...].astype(BF16)


def _class_rows(r):
    return pl.ds(r, CLASS_ROWS, stride=MAX_DILATION)


def _block_kernel(*refs, has_attn, has_proj, proj_scale, in_order, out_order):
    refs = list(refs)
    x_ref = refs.pop(0)
    if has_attn:
        a_ref, wo_src, g1_ref, b1_ref = refs[:4]
        del refs[:4]
    win_src, wout_src, g2_ref, b2_ref = refs[:4]
    del refs[:4]
    wp_src = refs.pop(0) if has_proj else None
    o_ref = refs.pop(0)
    p_ref = refs.pop(0) if has_proj else None
    win_ref, wout_ref = refs[:2]
    del refs[:2]
    weights = [(win_src, win_ref), (wout_src, wout_ref)]
    if has_attn:
        wo_ref = refs.pop(0)
        weights.append((wo_src, wo_ref))
    if has_proj:
        wp_ref = refs.pop(0)
        weights.append((wp_src, wp_ref))
    z_ref = refs.pop(0)
    step = pl.program_id(0)
    last = pl.num_programs(0) - 1

    @pl.when(step < W_STEPS)
    def _():
        _load_weight_chunks(step, weights)

    @pl.when(step == 0)
    def _():
        z_ref[...] = jnp.zeros_like(z_ref)

    def compute():
        d = x_ref.shape[-1]
        x = x_ref[...].reshape(ROW_TILE, d)
        if has_attn:
            a = a_ref[...].reshape(ROW_TILE, a_ref.shape[-1])
            xs = []
            for r0 in range(0, ROW_TILE, MIX_ROWS):
                rows = slice(r0, r0 + MIX_ROWS)
                mix = jnp.dot(a[rows].astype(BF16), wo_ref[...], preferred_element_type=F32)
                xs.append(_layer_norm(ALPHA * x[rows] + mix, g1_ref[...], b1_ref[...]))
            x = jnp.concatenate(xs, axis=0)
        h = jnp.dot(x.astype(BF16), win_ref[...], preferred_element_type=F32)
        act = h[:, :D_FF] * jax.nn.sigmoid(h[:, :D_FF]) * h[:, D_FF:]
        ffn = jnp.dot(act.astype(BF16), wout_ref[...], preferred_element_type=F32)
        z = ALPHA * x + 0.5 * ffn
        for j in range(N_SLABS):
            lanes = slice(j * LANES, (j + 1) * LANES)
            if in_order == "cm" and out_order == "nat":
                for r in range(MAX_DILATION):
                    z_ref[j, _class_rows(r), :] = z[r * CLASS_ROWS:(r + 1) * CLASS_ROWS, lanes]
            else:
                z_ref[j] = z[:, lanes]

    def finish():
        if in_order == "nat" and out_order == "cm":
            for r in range(MAX_DILATION):
                z = jnp.concatenate([z_ref[j, _class_rows(r), :] for j in range(N_SLABS)], axis=1)
                o_ref[r] = _layer_norm(z, g2_ref[...], b2_ref[...])
        else:
            z = jnp.concatenate([z_ref[j] for j in range(N_SLABS)], axis=1)
            y = _layer_norm(z, g2_ref[...], b2_ref[...])
            o_ref[...] = y.reshape(o_ref.shape)
        if has_proj:
            assert out_order == "nat"
            res = jnp.dot(y.astype(BF16), wp_ref[...], preferred_element_type=F32)
            if proj_scale is not None:
                res = res * proj_scale
            p_ref[...] = res.astype(p_ref.dtype)

    @pl.when((step >= W_STEPS) & (step < last))
    def _():
        finish()
        compute()

    @pl.when(step == last)
    def _():
        finish()


def _block(x, w_in, w_out, g, b, *, seq, in_order, out_order, attn=None, proj=None):
    t, d = x.shape
    n_tiles = t // ROW_TILE
    vec = lambda v: v.reshape(1, d)
    in_spec = lambda w: _tile_spec(in_order, seq, w, n_tiles)
    out_spec = lambda order, w: _tile_spec(order, seq, w, n_tiles, lag=1)
    args, specs = [_tile_view(x, in_order, seq)], [in_spec(d)]
    scratch = [_weight_scratch(w_in[0]), _weight_scratch(w_out[0])]
    if attn is not None:
        att, w_o, g1, b1 = attn
        args += [_tile_view(att, in_order, seq), w_o[0], vec(g1), vec(b1)]
        specs += [in_spec(d), _weight_chunk_spec(*w_o), _resident((1, d)), _resident((1, d))]
        scratch.append(_weight_scratch(w_o[0]))
    args += [w_in[0], w_out[0], vec(g), vec(b)]
    specs += [_weight_chunk_spec(*w_in), _weight_chunk_spec(*w_out),
              _resident((1, d)), _resident((1, d))]
    out_shapes = [jax.ShapeDtypeStruct(_tile_view(x, out_order, seq).shape, F32)]
    out_specs = [out_spec(out_order, d)]
    proj_scale = None
    if proj is not None:
        w_p, proj_scale, p_dtype = proj
        n_p = w_p[0].shape[-1]
        args.append(w_p[0])
        specs.append(_weight_chunk_spec(*w_p))
        scratch.append(_weight_scratch(w_p[0]))
        out_shapes.append(jax.ShapeDtypeStruct((t, n_p), p_dtype))
        out_specs.append(out_spec("nat", n_p))
    scratch.append(pltpu.VMEM((N_SLABS, ROW_TILE, LANES), F32))
    outs = pl.pallas_call(
        functools.partial(_block_kernel, has_attn=attn is not None, has_proj=proj is not None,
                          proj_scale=proj_scale, in_order=in_order, out_order=out_order),
        grid=(W_STEPS + n_tiles + 1,),
        in_specs=specs,
        out_specs=out_specs,
        out_shape=out_shapes,
        scratch_shapes=scratch,
        compiler_params=pltpu.CompilerParams(
            dimension_semantics=("arbitrary",), vmem_limit_bytes=VMEM_LIMIT),
        name="block_attn" if attn is not None else "block_ffn",
    )(*args)
    y = outs[0].reshape(t, d)
    return (y, outs[1]) if proj is not None else y


def _proj_kernel(x_ref, w_src, o_ref, w_ref, *, n_scaled):
    step = pl.program_id(0)

    @pl.when(step < W_STEPS)
    def _():
        _load_weight_chunks(step, [(w_src, w_ref)])

    @pl.when(step >= W_STEPS)
    def _():
        res = jnp.dot(x_ref[...].astype(BF16), w_ref[...], preferred_element_type=F32)
        if n_scaled:
            col = lax.broadcasted_iota(jnp.int32, (1, res.shape[1]), 1)
            res = res * jnp.where(col < n_scaled, Q_SCALE, 1.0)
        o_ref[...] = res.astype(o_ref.dtype)


def _proj(x, w, *, out_dtype, n_scaled=0):
    t, d = x.shape
    n = w[0].shape[-1]
    n_tiles = t // ROW_TILE
    return pl.pallas_call(
        functools.partial(_proj_kernel, n_scaled=n_scaled),
        grid=(W_STEPS + n_tiles,),
        in_specs=[_tile_spec("nat", 0, d, n_tiles), _weight_chunk_spec(*w)],
        out_specs=_tile_spec("nat", 0, n, n_tiles),
        out_shape=jax.ShapeDtypeStruct((t, n), out_dtype),
        scratch_shapes=[_weight_scratch(w[0])],
        compiler_params=pltpu.CompilerParams(
            dimension_semantics=("arbitrary",), vmem_limit_bytes=VMEM_LIMIT),
        name="proj",
    )(x, w[0])


def _attn_kernel(slope_ref, sink_ref, q_ref, k_ref, v_ref, o_ref, bias_ref, *scratch,
                 patterns, use_sinks):
    seq = q_ref.shape[0]
    blk = pl.program_id(0)
    multi = len(patterns) > 1
    if multi:
        acc_ref, max_ref, den_ref = scratch

    lane = lax.broadcasted_iota(jnp.int32, (BLOCK, LANES), 1)
    low_half = lane < HEAD_DIM
    head_sel = [jnp.where((lane >= h * HEAD_DIM) & (lane < (h + 1) * HEAD_DIM), 1.0, 0.0).astype(BF16)
                for h in range(HEADS_PER_BLOCK)]
    sel2 = [jnp.concatenate([sel, sel], axis=0) for sel in head_sel]

    @pl.when(pl.program_id(1) == 0)
    def _():
        row = lax.broadcasted_iota(jnp.int32, (BLOCK, 2 * BLOCK), 0)
        col = lax.broadcasted_iota(jnp.int32, (BLOCK, 2 * BLOCK), 1)
        in_cur = col >= BLOCK
        for pi, (n_chunks, _, max_dist, scale) in enumerate(patterns):
            chunk_len = BLOCK // n_chunks
            sh = chunk_len.bit_length() - 1
            pos = lambda x: n_chunks * (x & (chunk_len - 1)) + (x >> sh)
            dist = pos(row) - pos(col & (BLOCK - 1)) + jnp.where(in_cur, 0, BLOCK)
            valid = (dist >= 0) & (dist <= max_dist)
            for h in range(HEADS_PER_BLOCK):
                slope = slope_ref[blk * HEADS_PER_BLOCK + h]
                bias = jnp.where(valid, (-slope * scale * LOG2E) * dist.astype(F32), MASK_VALUE)
                bias_ref[(pi * 2 + h) * 2 + 0] = bias
                bias_ref[(pi * 2 + h) * 2 + 1] = jnp.where(in_cur, bias, MASK_VALUE)

    if use_sinks:
        sink2 = jnp.where(low_half, sink_ref[blk * HEADS_PER_BLOCK] * LOG2E,
                          sink_ref[blk * HEADS_PER_BLOCK + 1] * LOG2E)

    def block_task(pi, cls, n):
        n_chunks, n_classes, _, _ = patterns[pi]
        chunk_len = BLOCK // n_chunks
        class_rows = seq // (n_chunks * n_classes)
        static = isinstance(n, int)

        def chunks(n):
            starts = [(n_classes * c + cls) * class_rows + n * chunk_len for c in range(n_chunks)]
            return [pl.ds(s if static else pl.multiple_of(s, chunk_len), chunk_len)
                    for s in starts]

        def load(ref, slices):
            parts = [ref[s, :] for s in slices]
            return parts[0] if len(parts) == 1 else jnp.concatenate(parts, axis=0)

        def store(ref, slices, val):
            for c, s in enumerate(slices):
                part = val[c * chunk_len:(c + 1) * chunk_len]
                if multi:
                    ref[pi, s, :] = part
                else:
                    ref[s, :] = part.astype(ref.dtype)

        if static:
            first, prev_n = int(n == 0), max(n - 1, 0)
        else:
            first, prev_n = (n == 0).astype(jnp.int32), jnp.maximum(n - 1, 0)
        cur = chunks(n)
        prev = chunks(prev_n)
        q = load(q_ref, cur).astype(BF16)
        kk = jnp.concatenate([load(k_ref, prev), load(k_ref, cur)], axis=0).astype(BF16)
        vv = jnp.concatenate([load(v_ref, prev), load(v_ref, cur)], axis=0).astype(BF16)
        q2 = jnp.concatenate([q * head_sel[h] for h in range(HEADS_PER_BLOCK)], axis=0)
        s2 = lax.dot_general(q2, kk, (((1,), (1,)), ((), ())),
                             preferred_element_type=F32)
        ps, ms = [], []
        for h in range(HEADS_PER_BLOCK):
            s = s2[h * BLOCK:(h + 1) * BLOCK] + bias_ref[(pi * 2 + h) * 2 + first]
            m = jnp.max(s, axis=-1, keepdims=True)
            ps.append(jnp.exp2(s - m).astype(BF16))
            ms.append(m)
        rhs = jnp.concatenate(
            [jnp.concatenate([vv * sel2[h], sel2[h]], axis=1) for h in range(HEADS_PER_BLOCK)],
            axis=0)
        nd = jnp.dot(jnp.concatenate(ps, axis=1), rhs, preferred_element_type=F32)
        num, den = nd[:, :LANES], nd[:, LANES:]
        m2 = jnp.where(low_half, ms[0], ms[1])
        if multi:
            store(acc_ref, cur, num)
            store(max_ref, cur, m2)
            store(den_ref, cur, den)
        else:
            if use_sinks:
                den = den + jnp.exp2(sink2 - m2)
            store(o_ref, cur, num / den)

    def combine(row0, n_rows):
        rows = pl.ds(row0, n_rows)
        mxs = [max_ref[pi, rows, :] for pi in range(len(patterns))]
        mx = functools.reduce(jnp.maximum, mxs)
        es = [jnp.exp2(m - mx) for m in mxs]
        add = lambda a, b: a + b
        num = functools.reduce(add, [e * acc_ref[pi, rows, :] for pi, e in enumerate(es)])
        tot = functools.reduce(add, [e * den_ref[pi, rows, :] for pi, e in enumerate(es)])
        o_ref[rows, :] = (num / tot).astype(o_ref.dtype)

    for pi, (n_chunks, n_classes, _, _) in enumerate(patterns):
        class_rows = seq // (n_chunks * n_classes)
        blocks_per_class = class_rows // (BLOCK // n_chunks)
        shift = blocks_per_class.bit_length() - 1
        assert blocks_per_class == 1 << shift and n_classes * blocks_per_class == seq // BLOCK
        if multi and pi == len(patterns) - 1:
            assert n_chunks == 1
            for cls in range(n_classes):
                for n in range(blocks_per_class):
                    block_task(pi, cls, n)
                combine(cls * class_rows, class_rows)
        else:
            def task(t, carry, pi=pi, shift=shift, blocks_per_class=blocks_per_class):
                block_task(pi, lax.shift_right_logical(t, shift),
                           lax.bitwise_and(t, blocks_per_class - 1))
                return carry

            lax.fori_loop(0, seq // BLOCK, task, 0, unroll=True)


def _attention(q_arr, kv_arr, slopes, sinks, *, q_block, k_block, v_block, patterns, use_sinks):
    b, seq, _ = q_arr.shape
    n_blocks = N_HEADS // HEADS_PER_BLOCK
    n_pat = len(patterns)
    scratch = [pltpu.VMEM((n_pat * 4, BLOCK, 2 * BLOCK), F32)]
    if n_pat > 1:
        scratch += [pltpu.VMEM((n_pat, seq, LANES), F32) for _ in range(3)]
    smem = pl.BlockSpec(memory_space=pltpu.SMEM)
    return pl.pallas_call(
        functools.partial(_attn_kernel, patterns=patterns, use_sinks=use_sinks),
        grid=(n_blocks, b),
        in_specs=[
            smem, smem,
            pl.BlockSpec((None, seq, LANES), lambda j, i: (i, 0, q_block(j))),
            pl.BlockSpec((None, seq, LANES), lambda j, i: (i, 0, k_block(j))),
            pl.BlockSpec((None, seq, LANES), lambda j, i: (i, 0, v_block(j))),
        ],
        out_specs=pl.BlockSpec((None, seq, LANES), lambda j, i: (i, 0, j)),
        out_shape=jax.ShapeDtypeStruct((b, seq, D_MODEL), BF16),
        scratch_shapes=scratch,
        compiler_params=pltpu.CompilerParams(
            dimension_semantics=("arbitrary", "arbitrary"), vmem_limit_bytes=VMEM_LIMIT),
        name="attn_dilated" if n_pat > 1 else "attn_swa",
    )(slopes, sinks, q_arr, kv_arr, kv_arr)


def _alibi_slopes(n):
    return np.array([2.0 ** (-8.0 * (h + 1) / n) for h in range(n)], dtype=np.float32)


def _swa_head_order():
    order = []
    for c in range(N_KV_B // HEADS_PER_BLOCK):
        for g in range(GROUP_B):
            order += [(HEADS_PER_BLOCK * c) * GROUP_B + g, (HEADS_PER_BLOCK * c + 1) * GROUP_B + g]
    return np.array(order, dtype=np.int32)


def kernel(x, ffn1_w_in, ffn1_w_out, ffn2_w_in, ffn2_w_out, ln_g, ln_b,
           a_w_qkv, a_w_o, kv_w, b_w_q, b_sinks, b_w_o):
    b, seq, d = x.shape
    t = b * seq
    xs = x.reshape(t, d)
    slopes = _alibi_slopes(N_HEADS)
    no_sinks = jnp.zeros((N_HEADS,), F32)
    dil_patterns = tuple((MAX_DILATION // dil, dil, win // dil, float(dil))
                         for win, dil in DILATED_PATTERNS)
    swa_patterns = ((1, 1, WINDOW_B - 1, 1.0),)
    n_blocks = N_HEADS // HEADS_PER_BLOCK
    head_order = _swa_head_order()
    col_order = (head_order[:, None] * HEAD_DIM + np.arange(HEAD_DIM)[None, :]).reshape(-1)
    kv_blocks = N_KV_B // HEADS_PER_BLOCK
    kv = None
    order = "nat"
    for i in range(DEPTH):
        dilated = i < N_A_LAYERS
        j = i - N_A_LAYERS
        w_in, w_out = (ffn1_w_in, i), (ffn1_w_out, i)
        if dilated:
            xs = _block(xs, w_in, w_out, ln_g[i, 0], ln_b[i, 0], seq=seq,
                        in_order=order, out_order="cm")
            order = "cm"
            qkv = _proj(xs, (a_w_qkv, i), out_dtype=F32, n_scaled=d)
            qkv = qkv.reshape(b, seq, 3 * d)
            att = _attention(
                qkv, qkv, jnp.asarray(slopes), no_sinks,
                q_block=lambda jb: jb, k_block=lambda jb: n_blocks + jb,
                v_block=lambda jb: 2 * n_blocks + jb,
                patterns=dil_patterns, use_sinks=False)
            w_o = (a_w_o, i)
        else:
            xs, q = _block(xs, w_in, w_out, ln_g[i, 0], ln_b[i, 0], seq=seq,
                           in_order=order, out_order="nat",
                           proj=((b_w_q[j][:, col_order], None), Q_SCALE, BF16))
            order = "nat"
            att = _attention(
                q.reshape(b, seq, d), kv, jnp.asarray(slopes[head_order]),
                b_sinks[j][head_order],
                q_block=lambda jb: jb, k_block=lambda jb: jb // GROUP_B,
                v_block=lambda jb: kv_blocks + jb // GROUP_B,
                patterns=swa_patterns, use_sinks=True)
            w_o = (b_w_o[j][col_order, :], None)
        share_kv = i == N_A_LAYERS - 1
        out = _block(xs, (ffn2_w_in, i), (ffn2_w_out, i),
                     ln_g[i, 2], ln_b[i, 2], seq=seq, in_order=order, out_order="nat",
                     attn=(att.reshape(t, d), w_o, ln_g[i, 1], ln_b[i, 1]),
                     proj=((kv_w, None), None, BF16) if share_kv else None)
        order = "nat"
        if share_kv:
            xs, kv = out
            kv = kv.reshape(b, seq, 2 * N_KV_B * HEAD_DIM)
        else:
            xs = out
    return xs.reshape(b, seq, d)
```

```python
import functools
import math

import numpy as np
import jax
import jax.numpy as jnp
from jax import lax
from jax.experimental import pallas as pl
from jax.experimental.pallas import tpu as pltpu

D_MODEL = 1024
DEPTH = 2
HEAD_DIM = 64
N_HEADS = D_MODEL // HEAD_DIM
N_KV_B = 4
GROUP_B = N_HEADS // N_KV_B
D_FF = 2816
DILATED_PATTERNS = ((128, 1), (512, 4), (2048, 16))
MAX_DILATION = max(d for _, d in DILATED_PATTERNS)
WINDOW_B = 128
BLOCK = 128
N_A_LAYERS = DEPTH // 2
ALPHA = (2.0 * DEPTH) ** 0.25
LN_EPS = 1e-5

LANES = 128
HEADS_PER_BLOCK = LANES // HEAD_DIM
MASK_VALUE = -1e30
LOG2E = math.log2(math.e)
Q_SCALE = HEAD_DIM ** -0.5 * LOG2E
VMEM_LIMIT = 56 * 1024 * 1024
ROW_TILE = 512

BF16 = jnp.bfloat16
F32 = jnp.float32


def _layer_norm(z, g, b):
    mu = jnp.mean(z, axis=-1, keepdims=True)
    zc = z - mu
    var = jnp.mean(zc * zc, axis=-1, keepdims=True)
    return zc * lax.rsqrt(var + LN_EPS) * g + b


def _resident(shape):
    return pl.BlockSpec(shape, lambda *_: (0,) * len(shape), pipeline_mode=pl.Buffered(1))


CLASS_ROWS = ROW_TILE // MAX_DILATION
N_SLABS = D_MODEL // LANES
W_STEPS = 16


def _tile_spec(order, seq, w, n_tiles, lag=0):
    tile = lambda i: jnp.clip(i - W_STEPS - lag, 0, n_tiles - 1)
    if order == "nat":
        return pl.BlockSpec((ROW_TILE, w), lambda i: (tile(i), 0))
    tiles_per_batch = seq // ROW_TILE

    def index(i):
        i = tile(i)
        return (i // tiles_per_batch, 0, i % tiles_per_batch, 0)

    return pl.BlockSpec((None, MAX_DILATION, CLASS_ROWS, w), index)


def _tile_view(x, order, seq):
    t, w = x.shape
    if order == "nat":
        return x
    return x.reshape(t // seq, MAX_DILATION, seq // MAX_DILATION, w)


def _weight_chunk_spec(w, layer=None):
    rows, cols = w.shape[-2] // W_STEPS, w.shape[-1]
    chunk = lambda i: jnp.minimum(i, W_STEPS - 1)
    if layer is None:
        return pl.BlockSpec((rows, cols), lambda i: (chunk(i), 0))
    return pl.BlockSpec((None, rows, cols), lambda i: (layer, chunk(i), 0))


def _weight_scratch(w):
    return pltpu.VMEM(w.shape[-2:], BF16)


def _load_weight_chunks(step, pairs):
    for src_ref, dst_ref in pairs:
        rows = src_ref.shape[0]
        dst_ref[pl.ds(pl.multiple_of(step * rows, rows), rows), :] = src_ref[...].astype(BF16)


SUB = 4
SUB_ROWS = ROW_TILE // SUB
assert SUB * SUB == MAX_DILATION


def _block_kernel(*refs, has_attn, has_proj, proj_scale, in_order, out_order):
    refs = list(refs)
    x_ref = refs.pop(0)
    if has_attn:
        a_ref, wo_src, g1_ref, b1_ref = refs[:4]
        del refs[:4]
    win_src, wout_src, g2_ref, b2_ref = refs[:4]
    del refs[:4]
    wp_src = refs.pop(0) if has_proj else None
    o_ref = refs.pop(0)
    p_ref = refs.pop(0) if has_proj else None
    win_ref, wout_ref = refs[:2]
    del refs[:2]
    weights = [(win_src, win_ref), (wout_src, wout_ref)]
    if has_attn:
        wo_ref = refs.pop(0)
        weights.append((wo_src, wo_ref))
    if has_proj:
        wp_ref = refs.pop(0)
        weights.append((wp_src, wp_ref))
    slab_ref = refs.pop(0) if in_order != out_order else None
    z_ref = refs.pop(0)
    step = pl.program_id(0)
    last = pl.num_programs(0) - 1

    @pl.when(step < W_STEPS)
    def _():
        _load_weight_chunks(step, weights)

    @pl.when(step == 0)
    def _():
        z_ref[...] = jnp.zeros_like(z_ref)

    def compute():
        d = x_ref.shape[-1]
        x = x_ref[...].reshape(ROW_TILE, d)
        if has_attn:
            a = a_ref[...].reshape(ROW_TILE, a_ref.shape[-1])
            mix = jnp.dot(a.astype(BF16), wo_ref[...], preferred_element_type=F32)
            x = _layer_norm(ALPHA * x + mix, g1_ref[...], b1_ref[...])
        h = jnp.dot(x.astype(BF16), win_ref[...], preferred_element_type=F32)
        act = h[:, :D_FF] * jax.nn.sigmoid(h[:, :D_FF]) * h[:, D_FF:]
        ffn = jnp.dot(act.astype(BF16), wout_ref[...], preferred_element_type=F32)
        z_ref[...] = ALPHA * x + 0.5 * ffn

    def finish():
        y = _layer_norm(z_ref[...], g2_ref[...], b2_ref[...])
        if in_order == out_order:
            o_ref[...] = y.reshape(o_ref.shape)
        elif out_order == "cm":
            for j in range(N_SLABS):
                slab_ref[0, j] = y[:, j * LANES:(j + 1) * LANES]
            for j in range(N_SLABS):
                for r1 in range(SUB):
                    slab_ref[1, j, r1 * SUB_ROWS:(r1 + 1) * SUB_ROWS, :] = (
                        slab_ref[0, j, pl.ds(r1, SUB_ROWS, stride=SUB), :])
            for r1 in range(SUB):
                for r2 in range(SUB):
                    o_ref[SUB * r2 + r1] = jnp.concatenate(
                        [slab_ref[1, j, pl.ds(r1 * SUB_ROWS + r2, CLASS_ROWS, stride=SUB), :]
                         for j in range(N_SLABS)], axis=1)
        else:
            for r1 in range(SUB):
                for r2 in range(SUB):
                    r = SUB * r2 + r1
                    for j in range(N_SLABS):
                        slab_ref[1, j, pl.ds(r1 * SUB_ROWS + r2, CLASS_ROWS, stride=SUB), :] = (
                            y[r * CLASS_ROWS:(r + 1) * CLASS_ROWS, j * LANES:(j + 1) * LANES])
            for j in range(N_SLABS):
                for r1 in range(SUB):
                    slab_ref[0, j, pl.ds(r1, SUB_ROWS, stride=SUB), :] = (
                        slab_ref[1, j, r1 * SUB_ROWS:(r1 + 1) * SUB_ROWS, :])
            y = jnp.concatenate([slab_ref[0, j] for j in range(N_SLABS)], axis=1)
            o_ref[...] = y
        if has_proj:
            assert out_order == "nat"
            res = jnp.dot(y.astype(BF16), wp_ref[...], preferred_element_type=F32)
            if proj_scale is not None:
                res = res * proj_scale
            p_ref[...] = res.astype(p_ref.dtype)

    @pl.when((step >= W_STEPS) & (step < last))
    def _():
        finish()
        compute()

    @pl.when(step == last)
    def _():
        finish()


def _block(x, w_in, w_out, g, b, *, seq, in_order, out_order, attn=None, proj=None):
    t, d = x.shape
    n_tiles = t // ROW_TILE
    vec = lambda v: v.reshape(1, d)
    in_spec = lambda w: _tile_spec(in_order, seq, w, n_tiles)
    out_spec = lambda order, w: _tile_spec(order, seq, w, n_tiles, lag=1)
    args, specs = [_tile_view(x, in_order, seq)], [in_spec(d)]
    scratch = [_weight_scratch(w_in[0]), _weight_scratch(w_out[0])]
    if attn is not None:
        att, w_o, g1, b1 = attn
        args += [_tile_view(att, in_order, seq), w_o[0], vec(g1), vec(b1)]
        specs += [in_spec(d), _weight_chunk_spec(*w_o), _resident((1, d)), _resident((1, d))]
        scratch.append(_weight_scratch(w_o[0]))
    args += [w_in[0], w_out[0], vec(g), vec(b)]
    specs += [_weight_chunk_spec(*w_in), _weight_chunk_spec(*w_out),
              _resident((1, d)), _resident((1, d))]
    out_shapes = [jax.ShapeDtypeStruct(_tile_view(x, out_order, seq).shape, F32)]
    out_specs = [out_spec(out_order, d)]
    proj_scale = None
    if proj is not None:
        w_p, proj_scale, p_dtype = proj
        n_p = w_p[0].shape[-1]
        args.append(w_p[0])
        specs.append(_weight_chunk_spec(*w_p))
        scratch.append(_weight_scratch(w_p[0]))
        out_shapes.append(jax.ShapeDtypeStruct((t, n_p), p_dtype))
        out_specs.append(out_spec("nat", n_p))
    if in_order != out_order:
        scratch.append(pltpu.VMEM((2, N_SLABS, ROW_TILE, LANES), F32))
    scratch.append(pltpu.VMEM((ROW_TILE, d), F32))
    outs = pl.pallas_call(
        functools.partial(_block_kernel, has_attn=attn is not None, has_proj=proj is not None,
                          proj_scale=proj_scale, in_order=in_order, out_order=out_order),
        grid=(W_STEPS + n_tiles + 1,),
        in_specs=specs,
        out_specs=out_specs,
        out_shape=out_shapes,
        scratch_shapes=scratch,
        compiler_params=pltpu.CompilerParams(
            dimension_semantics=("arbitrary",), vmem_limit_bytes=VMEM_LIMIT),
        name="block_attn" if attn is not None else "block_ffn",
    )(*args)
    y = outs[0].reshape(t, d)
    return (y, outs[1]) if proj is not None else y


def _proj_kernel(x_ref, w_src, o_ref, w_ref, *, n_scaled):
    step = pl.program_id(0)

    @pl.when(step < W_STEPS)
    def _():
        _load_weight_chunks(step, [(w_src, w_ref)])

    @pl.when(step >= W_STEPS)
    def _():
        res = jnp.dot(x_ref[...].astype(BF16), w_ref[...], preferred_element_type=F32)
        if n_scaled:
            col = lax.broadcasted_iota(jnp.int32, (1, res.shape[1]), 1)
            res = res * jnp.where(col < n_scaled, Q_SCALE, 1.0)
        o_ref[...] = res.astype(o_ref.dtype)


def _proj(x, w, *, out_dtype, n_scaled=0):
    t, d = x.shape
    n = w[0].shape[-1]
    n_tiles = t // ROW_TILE
    return pl.pallas_call(
        functools.partial(_proj_kernel, n_scaled=n_scaled),
        grid=(W_STEPS + n_tiles,),
        in_specs=[_tile_spec("nat", 0, d, n_tiles), _weight_chunk_spec(*w)],
        out_specs=_tile_spec("nat", 0, n, n_tiles),
        out_shape=jax.ShapeDtypeStruct((t, n), out_dtype),
        scratch_shapes=[_weight_scratch(w[0])],
        compiler_params=pltpu.CompilerParams(
            dimension_semantics=("arbitrary",), vmem_limit_bytes=VMEM_LIMIT),
        name="proj",
    )(x, w[0])


def _attn_kernel(slope_ref, sink_ref, q_ref, k_ref, v_ref, o_ref, bias_ref, *scratch,
                 patterns, use_sinks):
    seq = q_ref.shape[0]
    blk = pl.program_id(0)
    multi = len(patterns) > 1
    if multi:
        acc_ref, max_ref, den_ref = scratch

    lane = lax.broadcasted_iota(jnp.int32, (BLOCK, LANES), 1)
    low_half = lane < HEAD_DIM
    head_sel = [jnp.where((lane >= h * HEAD_DIM) & (lane < (h + 1) * HEAD_DIM), 1.0, 0.0).astype(BF16)
                for h in range(HEADS_PER_BLOCK)]
    sel2 = [jnp.concatenate([sel, sel], axis=0) for sel in head_sel]

    @pl.when(pl.program_id(1) == 0)
    def _():
        row = lax.broadcasted_iota(jnp.int32, (BLOCK, 2 * BLOCK), 0)
        col = lax.broadcasted_iota(jnp.int32, (BLOCK, 2 * BLOCK), 1)
        in_cur = col >= BLOCK
        for pi, (n_chunks, _, max_dist, scale) in enumerate(patterns):
            chunk_len = BLOCK // n_chunks
            sh = chunk_len.bit_length() - 1
            pos = lambda x: n_chunks * (x & (chunk_len - 1)) + (x >> sh)
            dist = pos(row) - pos(col & (BLOCK - 1)) + jnp.where(in_cur, 0, BLOCK)
            valid = (dist >= 0) & (dist <= max_dist)
            for h in range(HEADS_PER_BLOCK):
                slope = slope_ref[blk * HEADS_PER_BLOCK + h]
                bias = jnp.where(valid, (-slope * scale * LOG2E) * dist.astype(F32), MASK_VALUE)
                bias_ref[(pi * 2 + h) * 2 + 0] = bias
                bias_ref[(pi * 2 + h) * 2 + 1] = jnp.where(in_cur, bias, MASK_VALUE)

    if use_sinks:
        sink2 = jnp.where(low_half, sink_ref[blk * HEADS_PER_BLOCK] * LOG2E,
                          sink_ref[blk * HEADS_PER_BLOCK + 1] * LOG2E)

    def block_task(pi, cls, n):
        n_chunks, n_classes, _, _ = patterns[pi]
        chunk_len = BLOCK // n_chunks
        class_rows = seq // (n_chunks * n_classes)
        static = isinstance(n, int)

        def chunks(n):
            starts = [(n_classes * c + cls) * class_rows + n * chunk_len for c in range(n_chunks)]
            return [pl.ds(s if static else pl.multiple_of(s, chunk_len), chunk_len)
                    for s in starts]

        def load(ref, slices):
            parts = [ref[s, :] for s in slices]
            return parts[0] if len(parts) == 1 else jnp.concatenate(parts, axis=0)

        def store(ref, slices, val):
            for c, s in enumerate(slices):
                part = val[c * chunk_len:(c + 1) * chunk_len]
                if multi:
                    ref[pi, s, :] = part
                else:
                    ref[s, :] = part.astype(ref.dtype)

        if static:
            first, prev_n = int(n == 0), max(n - 1, 0)
        else:
            first, prev_n = (n == 0).astype(jnp.int32), jnp.maximum(n - 1, 0)
        cur = chunks(n)
        prev = chunks(prev_n)
        q = load(q_ref, cur).astype(BF16)
        kk = jnp.concatenate([load(k_ref, prev), load(k_ref, cur)], axis=0).astype(BF16)
        vv = jnp.concatenate([load(v_ref, prev), load(v_ref, cur)], axis=0).astype(BF16)
        q2 = jnp.concatenate([q * head_sel[h] for h in range(HEADS_PER_BLOCK)], axis=0)
        s2 = lax.dot_general(q2, kk, (((1,), (1,)), ((), ())),
                             preferred_element_type=F32)
        ps, ms = [], []
        for h in range(HEADS_PER_BLOCK):
            s = s2[h * BLOCK:(h + 1) * BLOCK] + bias_ref[(pi * 2 + h) * 2 + first]
            m = jnp.max(s, axis=-1, keepdims=True)
            ps.append(jnp.exp2(s - m).astype(BF16))
            ms.append(m)
        rhs = jnp.concatenate(
            [jnp.concatenate([vv * sel2[h], sel2[h]], axis=1) for h in range(HEADS_PER_BLOCK)],
            axis=0)
        nd = jnp.dot(jnp.concatenate(ps, axis=1), rhs, preferred_element_type=F32)
        num, den = nd[:, :LANES], nd[:, LANES:]
        m2 = jnp.where(low_half, ms[0], ms[1])
        if multi:
            store(acc_ref, cur, num)
            store(max_ref, cur, m2)
            store(den_ref, cur, den)
        else:
            if use_sinks:
                den = den + jnp.exp2(sink2 - m2)
            store(o_ref, cur, num / den)

    def combine(row0, n_rows):
        rows = pl.ds(row0, n_rows)
        mxs = [max_ref[pi, rows, :] for pi in range(len(patterns))]
        mx = functools.reduce(jnp.maximum, mxs)
        es = [jnp.exp2(m - mx) for m in mxs]
        add = lambda a, b: a + b
        num = functools.reduce(add, [e * acc_ref[pi, rows, :] for pi, e in enumerate(es)])
        tot = functools.reduce(add, [e * den_ref[pi, rows, :] for pi, e in enumerate(es)])
        o_ref[rows, :] = (num / tot).astype(o_ref.dtype)

    for pi, (n_chunks, n_classes, _, _) in enumerate(patterns):
        class_rows = seq // (n_chunks * n_classes)
        blocks_per_class = class_rows // (BLOCK // n_chunks)
        shift = blocks_per_class.bit_length() - 1
        assert blocks_per_class == 1 << shift and n_classes * blocks_per_class == seq // BLOCK
        if multi and pi == len(patterns) - 1:
            assert n_chunks == 1
            for cls in range(n_classes):
                for n in range(blocks_per_class):
                    block_task(pi, cls, n)
                combine(cls * class_rows, class_rows)
        else:
            def task(t, carry, pi=pi, shift=shift, blocks_per_class=blocks_per_class):
                block_task(pi, lax.shift_right_logical(t, shift),
                           lax.bitwise_and(t, blocks_per_class - 1))
                return carry

            lax.fori_loop(0, seq // BLOCK, task, 0, unroll=True)


def _attention(q_arr, kv_arr, slopes, sinks, *, q_block, k_block, v_block, patterns, use_sinks):
    b, seq, _ = q_arr.shape
    n_blocks = N_HEADS // HEADS_PER_BLOCK
    n_pat = len(patterns)
    scratch = [pltpu.VMEM((n_pat * 4, BLOCK, 2 * BLOCK), F32)]
    if n_pat > 1:
        scratch += [pltpu.VMEM((n_pat, seq, LANES), F32) for _ in range(3)]
    smem = pl.BlockSpec(memory_space=pltpu.SMEM)
    return pl.pallas_call(
        functools.partial(_attn_kernel, patterns=patterns, use_sinks=use_sinks),
        grid=(n_blocks, b),
        in_specs=[
            smem, smem,
            pl.BlockSpec((None, seq, LANES), lambda j, i: (i, 0, q_block(j))),
            pl.BlockSpec((None, seq, LANES), lambda j, i: (i, 0, k_block(j))),
            pl.BlockSpec((None, seq, LANES), lambda j, i: (i, 0, v_block(j))),
        ],
        out_specs=pl.BlockSpec((None, seq, LANES), lambda j, i: (i, 0, j)),
        out_shape=jax.ShapeDtypeStruct((b, seq, D_MODEL), BF16),
        scratch_shapes=scratch,
        compiler_params=pltpu.CompilerParams(
            dimension_semantics=("arbitrary", "arbitrary"), vmem_limit_bytes=VMEM_LIMIT),
        name="attn_dilated" if n_pat > 1 else "attn_swa",
    )(slopes, sinks, q_arr, kv_arr, kv_arr)


def _alibi_slopes(n):
    return np.array([2.0 ** (-8.0 * (h + 1) / n) for h in range(n)], dtype=np.float32)


def _swa_head_order():
    order = []
    for c in range(N_KV_B // HEADS_PER_BLOCK):
        for g in range(GROUP_B):
            order += [(HEADS_PER_BLOCK * c) * GROUP_B + g, (HEADS_PER_BLOCK * c + 1) * GROUP_B + g]
    return np.array(order, dtype=np.int32)


def kernel(x, ffn1_w_in, ffn1_w_out, ffn2_w_in, ffn2_w_out, ln_g, ln_b,
           a_w_qkv, a_w_o, kv_w, b_w_q, b_sinks, b_w_o):
    b, seq, d = x.shape
    t = b * seq
    xs = x.reshape(t, d)
    slopes = _alibi_slopes(N_HEADS)
    no_sinks = jnp.zeros((N_HEADS,), F32)
    dil_patterns = tuple((MAX_DILATION // dil, dil, win // dil, float(dil))
                         for win, dil in DILATED_PATTERNS)
    swa_patterns = ((1, 1, WINDOW_B - 1, 1.0),)
    n_blocks = N_HEADS // HEADS_PER_BLOCK
    head_order = _swa_head_order()
    col_order = (head_order[:, None] * HEAD_DIM + np.arange(HEAD_DIM)[None, :]).reshape(-1)
    kv_blocks = N_KV_B // HEADS_PER_BLOCK
    kv = None
    order = "nat"
    for i in range(DEPTH):
        dilated = i < N_A_LAYERS
        j = i - N_A_LAYERS
        w_in, w_out = (ffn1_w_in, i), (ffn1_w_out, i)
        if dilated:
            xs = _block(xs, w_in, w_out, ln_g[i, 0], ln_b[i, 0], seq=seq,
                        in_order=order, out_order="cm")
            order = "cm"
            qkv = _proj(xs, (a_w_qkv, i), out_dtype=F32, n_scaled=d)
            qkv = qkv.reshape(b, seq, 3 * d)
            att = _attention(
                qkv, qkv, jnp.asarray(slopes), no_sinks,
                q_block=lambda jb: jb, k_block=lambda jb: n_blocks + jb,
                v_block=lambda jb: 2 * n_blocks + jb,
                patterns=dil_patterns, use_sinks=False)
            w_o = (a_w_o, i)
        else:
            xs, q = _block(xs, w_in, w_out, ln_g[i, 0], ln_b[i, 0], seq=seq,
                           in_order=order, out_order="nat",
                           proj=((b_w_q[j][:, col_order], None), Q_SCALE, BF16))
            order = "nat"
            att = _attention(
                q.reshape(b, seq, d), kv, jnp.asarray(slopes[head_order]),
                b_sinks[j][head_order],
                q_block=lambda jb: jb, k_block=lambda jb: jb // GROUP_B,
                v_block=lambda jb: kv_blocks + jb // GROUP_B,
                patterns=swa_patterns, use_sinks=True)
            w_o = (b_w_o[j][col_order, :], None)
        share_kv = i == N_A_LAYERS - 1
        out = _block(xs, (ffn2_w_in, i), (ffn2_w_out, i),
                     ln_g[i, 2], ln_b[i, 2], seq=seq, in_order=order, out_order="nat",
                     attn=(att.reshape(t, d), w_o, ln_g[i, 1], ln_b[i, 1]),
                     proj=((kv_w, None), None, BF16) if share_kv else None)
        order = "nat"
        if share_kv:
            xs, kv = out
            kv = kv.reshape(b, seq, 2 * N_KV_B * HEAD_DIM)
        else:
            xs = out
    return xs.reshape(b, seq, d)
```

```python
import functools
import math

import numpy as np
import jax
import jax.numpy as jnp
from jax import lax
from jax.experimental import pallas as pl
from jax.experimental.pallas import tpu as pltpu

D_MODEL = 1024
DEPTH = 2
HEAD_DIM = 64
N_HEADS = D_MODEL // HEAD_DIM
N_KV_B = 4
GROUP_B = N_HEADS // N_KV_B
D_FF = 2816
DILATED_PATTERNS = ((128, 1), (512, 4), (2048, 16))
MAX_DILATION = max(d for _, d in DILATED_PATTERNS)
WINDOW_B = 128
BLOCK = 128
N_A_LAYERS = DEPTH // 2
ALPHA = (2.0 * DEPTH) ** 0.25
LN_EPS = 1e-5

LANES = 128
HEADS_PER_BLOCK = LANES // HEAD_DIM
MASK_VALUE = -1e30
LOG2E = math.log2(math.e)
Q_SCALE = HEAD_DIM ** -0.5 * LOG2E
VMEM_LIMIT = 56 * 1024 * 1024
ROW_TILE = 512

BF16 = jnp.bfloat16
F32 = jnp.float32


def _layer_norm(z, g, b):
    mu = jnp.mean(z, axis=-1, keepdims=True)
    zc = z - mu
    var = jnp.mean(zc * zc, axis=-1, keepdims=True)
    return zc * lax.rsqrt(var + LN_EPS) * g + b


def _resident(shape):
    return pl.BlockSpec(shape, lambda *_: (0,) * len(shape), pipeline_mode=pl.Buffered(1))


CLASS_ROWS = ROW_TILE // MAX_DILATION
N_SLABS = D_MODEL // LANES
W_STEPS = 16


def _tile_spec(order, seq, w, n_tiles, lag=0):
    tile = lambda i: jnp.clip(i - W_STEPS - lag, 0, n_tiles - 1)
    if order == "nat":
        return pl.BlockSpec((ROW_TILE, w), lambda i: (tile(i), 0))
    tiles_per_batch = seq // ROW_TILE

    def index(i):
        i = tile(i)
        return (i // tiles_per_batch, 0, i % tiles_per_batch, 0)

    return pl.BlockSpec((None, MAX_DILATION, CLASS_ROWS, w), index)


def _tile_view(x, order, seq):
    t, w = x.shape
    if order == "nat":
        return x
    return x.reshape(t // seq, MAX_DILATION, seq // MAX_DILATION, w)


def _weight_chunk_spec(w, layer=None):
    rows, cols = w.shape[-2] // W_STEPS, w.shape[-1]
    chunk = lambda i: jnp.minimum(i, W_STEPS - 1)
    if layer is None:
        return pl.BlockSpec((rows, cols), lambda i: (chunk(i), 0))
    return pl.BlockSpec((None, rows, cols), lambda i: (layer, chunk(i), 0))


def _weight_scratch(w):
    return pltpu.VMEM(w.shape[-2:], BF16)


def _load_weight_chunks(step, pairs, interleave=()):
    for src_ref, dst_ref in pairs:
        rows, cols = src_ref.shape
        dst_rows = pl.ds(pl.multiple_of(step * rows, rows), rows)
        if any(src_ref is ref for ref in interleave):
            half = cols // 2
            for k in range(half // LANES):
                lanes = slice(k * LANES, (k + 1) * LANES)
                dst_ref[dst_rows, 2 * k * LANES:(2 * k + 1) * LANES] = (
                    src_ref[:, lanes].astype(BF16))
                dst_ref[dst_rows, (2 * k + 1) * LANES:(2 * k + 2) * LANES] = (
                    src_ref[:, half + k * LANES:half + (k + 1) * LANES].astype(BF16))
        else:
            dst_ref[dst_rows, :] = src_ref[...].astype(BF16)


SUB = 4
SUB_ROWS = ROW_TILE // SUB
assert SUB * SUB == MAX_DILATION


def _block_kernel(*refs, has_attn, has_proj, proj_scale, in_order, out_order):
    refs = list(refs)
    x_ref = refs.pop(0)
    if has_attn:
        a_ref, wo_src, g1_ref, b1_ref = refs[:4]
        del refs[:4]
    win_src, wout_src, g2_ref, b2_ref = refs[:4]
    del refs[:4]
    wp_src = refs.pop(0) if has_proj else None
    o_ref = refs.pop(0)
    p_ref = refs.pop(0) if has_proj else None
    win_ref, wout_ref = refs[:2]
    del refs[:2]
    weights = [(win_src, win_ref), (wout_src, wout_ref)]
    if has_attn:
        wo_ref = refs.pop(0)
        weights.append((wo_src, wo_ref))
    if has_proj:
        wp_ref = refs.pop(0)
        weights.append((wp_src, wp_ref))
    slab_ref = refs.pop(0) if in_order != out_order else None
    z_ref = refs.pop(0)
    step = pl.program_id(0)
    last = pl.num_programs(0) - 1

    @pl.when(step < W_STEPS)
    def _():
        _load_weight_chunks(step, weights, interleave=(win_src,))

    @pl.when(step == 0)
    def _():
        z_ref[...] = jnp.zeros_like(z_ref)

    def compute():
        d = x_ref.shape[-1]
        x = x_ref[...].reshape(ROW_TILE, d)
        if has_attn:
            a = a_ref[...].reshape(ROW_TILE, a_ref.shape[-1])
            mix = jnp.dot(a.astype(BF16), wo_ref[...], preferred_element_type=F32)
            x = _layer_norm(ALPHA * x + mix, g1_ref[...], b1_ref[...])
        h = jnp.dot(x.astype(BF16), win_ref[...], preferred_element_type=F32)
        acts = []
        for k in range(D_FF // LANES):
            gate = h[:, 2 * k * LANES:(2 * k + 1) * LANES]
            up = h[:, (2 * k + 1) * LANES:(2 * k + 2) * LANES]
            acts.append((gate * jax.nn.sigmoid(gate) * up).astype(BF16))
        act = jnp.concatenate(acts, axis=1)
        ffn = jnp.dot(act, wout_ref[...], preferred_element_type=F32)
        z_ref[...] = ALPHA * x + 0.5 * ffn

    def finish():
        y = _layer_norm(z_ref[...], g2_ref[...], b2_ref[...])
        if in_order == out_order:
            o_ref[...] = y.reshape(o_ref.shape)
        elif out_order == "cm":
            for j in range(N_SLABS):
                slab_ref[0, j] = y[:, j * LANES:(j + 1) * LANES]
            for j in range(N_SLABS):
                for r1 in range(SUB):
                    slab_ref[1, j, r1 * SUB_ROWS:(r1 + 1) * SUB_ROWS, :] = (
                        slab_ref[0, j, pl.ds(r1, SUB_ROWS, stride=SUB), :])
            for r1 in range(SUB):
                for r2 in range(SUB):
                    o_ref[SUB * r2 + r1] = jnp.concatenate(
                        [slab_ref[1, j, pl.ds(r1 * SUB_ROWS + r2, CLASS_ROWS, stride=SUB), :]
                         for j in range(N_SLABS)], axis=1)
        else:
            for r1 in range(SUB):
                for r2 in range(SUB):
                    r = SUB * r2 + r1
                    for j in range(N_SLABS):
                        slab_ref[1, j, pl.ds(r1 * SUB_ROWS + r2, CLASS_ROWS, stride=SUB), :] = (
                            y[r * CLASS_ROWS:(r + 1) * CLASS_ROWS, j * LANES:(j + 1) * LANES])
            for j in range(N_SLABS):
                for r1 in range(SUB):
                    slab_ref[0, j, pl.ds(r1, SUB_ROWS, stride=SUB), :] = (
                        slab_ref[1, j, r1 * SUB_ROWS:(r1 + 1) * SUB_ROWS, :])
            y = jnp.concatenate([slab_ref[0, j] for j in range(N_SLABS)], axis=1)
            o_ref[...] = y
        if has_proj:
            assert out_order == "nat"
            res = jnp.dot(y.astype(BF16), wp_ref[...], preferred_element_type=F32)
            if proj_scale is not None:
                res = res * proj_scale
            p_ref[...] = res.astype(p_ref.dtype)

    @pl.when((step >= W_STEPS) & (step < last))
    def _():
        finish()
        compute()

    @pl.when(step == last)
    def _():
        finish()


def _block(x, w_in, w_out, g, b, *, seq, in_order, out_order, attn=None, proj=None):
    t, d = x.shape
    n_tiles = t // ROW_TILE
    vec = lambda v: v.reshape(1, d)
    in_spec = lambda w: _tile_spec(in_order, seq, w, n_tiles)
    out_spec = lambda order, w: _tile_spec(order, seq, w, n_tiles, lag=1)
    args, specs = [_tile_view(x, in_order, seq)], [in_spec(d)]
    scratch = [_weight_scratch(w_in[0]), _weight_scratch(w_out[0])]
    if attn is not None:
        att, w_o, g1, b1 = attn
        args += [_tile_view(att, in_order, seq), w_o[0], vec(g1), vec(b1)]
        specs += [in_spec(d), _weight_chunk_spec(*w_o), _resident((1, d)), _resident((1, d))]
        scratch.append(_weight_scratch(w_o[0]))
    args += [w_in[0], w_out[0], vec(g), vec(b)]
    specs += [_weight_chunk_spec(*w_in), _weight_chunk_spec(*w_out),
              _resident((1, d)), _resident((1, d))]
    out_shapes = [jax.ShapeDtypeStruct(_tile_view(x, out_order, seq).shape, F32)]
    out_specs = [out_spec(out_order, d)]
    proj_scale = None
    if proj is not None:
        w_p, proj_scale, p_dtype = proj
        n_p = w_p[0].shape[-1]
        args.append(w_p[0])
        specs.append(_weight_chunk_spec(*w_p))
        scratch.append(_weight_scratch(w_p[0]))
        out_shapes.append(jax.ShapeDtypeStruct((t, n_p), p_dtype))
        out_specs.append(out_spec("nat", n_p))
    if in_order != out_order:
        scratch.append(pltpu.VMEM((2, N_SLABS, ROW_TILE, LANES), F32))
    scratch.append(pltpu.VMEM((ROW_TILE, d), F32))
    outs = pl.pallas_call(
        functools.partial(_block_kernel, has_attn=attn is not None, has_proj=proj is not None,
                          proj_scale=proj_scale, in_order=in_order, out_order=out_order),
        grid=(W_STEPS + n_tiles + 1,),
        in_specs=specs,
        out_specs=out_specs,
        out_shape=out_shapes,
        scratch_shapes=scratch,
        compiler_params=pltpu.CompilerParams(
            dimension_semantics=("arbitrary",), vmem_limit_bytes=VMEM_LIMIT),
        name="block_attn" if attn is not None else "block_ffn",
    )(*args)
    y = outs[0].reshape(t, d)
    return (y, outs[1]) if proj is not None else y


def _proj_kernel(x_ref, w_src, o_ref, w_ref, *, n_scaled):
    step = pl.program_id(0)

    @pl.when(step < W_STEPS)
    def _():
        _load_weight_chunks(step, [(w_src, w_ref)])

    @pl.when(step >= W_STEPS)
    def _():
        res = jnp.dot(x_ref[...].astype(BF16), w_ref[...], preferred_element_type=F32)
        if n_scaled:
            col = lax.broadcasted_iota(jnp.int32, (1, res.shape[1]), 1)
            res = res * jnp.where(col < n_scaled, Q_SCALE, 1.0)
        o_ref[...] = res.astype(o_ref.dtype)


def _proj(x, w, *, out_dtype, n_scaled=0):
    t, d = x.shape
    n = w[0].shape[-1]
    n_tiles = t // ROW_TILE
    return pl.pallas_call(
        functools.partial(_proj_kernel, n_scaled=n_scaled),
        grid=(W_STEPS + n_tiles,),
        in_specs=[_tile_spec("nat", 0, d, n_tiles), _weight_chunk_spec(*w)],
        out_specs=_tile_spec("nat", 0, n, n_tiles),
        out_shape=jax.ShapeDtypeStruct((t, n), out_dtype),
        scratch_shapes=[_weight_scratch(w[0])],
        compiler_params=pltpu.CompilerParams(
            dimension_semantics=("arbitrary",), vmem_limit_bytes=VMEM_LIMIT),
        name="proj",
    )(x, w[0])


def _attn_kernel(slope_ref, sink_ref, q_ref, k_ref, v_ref, o_ref, bias_ref, *scratch,
                 patterns, use_sinks):
    seq = q_ref.shape[0]
    blk = pl.program_id(0)
    multi = len(patterns) > 1
    if multi:
        acc_ref, max_ref, den_ref = scratch

    lane = lax.broadcasted_iota(jnp.int32, (BLOCK, LANES), 1)
    low_half = lane < HEAD_DIM
    head_sel = [jnp.where((lane >= h * HEAD_DIM) & (lane < (h + 1) * HEAD_DIM), 1.0, 0.0).astype(BF16)
                for h in range(HEADS_PER_BLOCK)]
    sel2 = [jnp.concatenate([sel, sel], axis=0) for sel in head_sel]

    @pl.when(pl.program_id(1) == 0)
    def _():
        row = lax.broadcasted_iota(jnp.int32, (BLOCK, 2 * BLOCK), 0)
        col = lax.broadcasted_iota(jnp.int32, (BLOCK, 2 * BLOCK), 1)
        in_cur = col >= BLOCK
        for pi, (n_chunks, _, max_dist, scale) in enumerate(patterns):
            chunk_len = BLOCK // n_chunks
            sh = chunk_len.bit_length() - 1
            pos = lambda x: n_chunks * (x & (chunk_len - 1)) + (x >> sh)
            dist = pos(row) - pos(col & (BLOCK - 1)) + jnp.where(in_cur, 0, BLOCK)
            valid = (dist >= 0) & (dist <= max_dist)
            for h in range(HEADS_PER_BLOCK):
                slope = slope_ref[blk * HEADS_PER_BLOCK + h]
                bias = jnp.where(valid, (-slope * scale * LOG2E) * dist.astype(F32), MASK_VALUE)
                bias_ref[(pi * 2 + h) * 2 + 0] = bias
                bias_ref[(pi * 2 + h) * 2 + 1] = jnp.where(in_cur, bias, MASK_VALUE)

    if use_sinks:
        sink2 = jnp.where(low_half, sink_ref[blk * HEADS_PER_BLOCK] * LOG2E,
                          sink_ref[blk * HEADS_PER_BLOCK + 1] * LOG2E)

    def block_task(pi, cls, n):
        n_chunks, n_classes, _, _ = patterns[pi]
        chunk_len = BLOCK // n_chunks
        class_rows = seq // (n_chunks * n_classes)
        static = isinstance(n, int)

        def chunks(n):
            starts = [(n_classes * c + cls) * class_rows + n * chunk_len for c in range(n_chunks)]
            return [pl.ds(s if static else pl.multiple_of(s, chunk_len), chunk_len)
                    for s in starts]

        def load(ref, slices):
            parts = [ref[s, :] for s in slices]
            return parts[0] if len(parts) == 1 else jnp.concatenate(parts, axis=0)

        def store(ref, slices, val):
            for c, s in enumerate(slices):
                part = val[c * chunk_len:(c + 1) * chunk_len]
                if multi:
                    ref[pi, s, :] = part
                else:
                    ref[s, :] = part.astype(ref.dtype)

        if static:
            first, prev_n = int(n == 0), max(n - 1, 0)
        else:
            first, prev_n = (n == 0).astype(jnp.int32), jnp.maximum(n - 1, 0)
        cur = chunks(n)
        prev = chunks(prev_n)
        q = load(q_ref, cur).astype(BF16)
        kk = jnp.concatenate([load(k_ref, prev), load(k_ref, cur)], axis=0).astype(BF16)
        vv = jnp.concatenate([load(v_ref, prev), load(v_ref, cur)], axis=0).astype(BF16)
        q2 = jnp.concatenate([q * head_sel[h] for h in range(HEADS_PER_BLOCK)], axis=0)
        s2 = lax.dot_general(q2, kk, (((1,), (1,)), ((), ())),
                             preferred_element_type=F32)
        ps, ms = [], []
        for h in range(HEADS_PER_BLOCK):
            s = s2[h * BLOCK:(h + 1) * BLOCK] + bias_ref[(pi * 2 + h) * 2 + first]
            m = jnp.max(s, axis=-1, keepdims=True)
            ps.append(jnp.exp2(s - m).astype(BF16))
            ms.append(m)
        rhs = jnp.concatenate(
            [jnp.concatenate([vv * sel2[h], sel2[h]], axis=1) for h in range(HEADS_PER_BLOCK)],
            axis=0)
        nd = jnp.dot(jnp.concatenate(ps, axis=1), rhs, preferred_element_type=F32)
        num, den = nd[:, :LANES], nd[:, LANES:]
        m2 = jnp.where(low_half, ms[0], ms[1])
        if multi:
            store(acc_ref, cur, num)
            store(max_ref, cur, m2)
            store(den_ref, cur, den)
        else:
            if use_sinks:
                den = den + jnp.exp2(sink2 - m2)
            store(o_ref, cur, num / den)

    def combine(row0, n_rows):
        rows = pl.ds(row0, n_rows)
        mxs = [max_ref[pi, rows, :] for pi in range(len(patterns))]
        mx = functools.reduce(jnp.maximum, mxs)
        es = [jnp.exp2(m - mx) for m in mxs]
        add = lambda a, b: a + b
        num = functools.reduce(add, [e * acc_ref[pi, rows, :] for pi, e in enumerate(es)])
        tot = functools.reduce(add, [e * den_ref[pi, rows, :] for pi, e in enumerate(es)])
        o_ref[rows, :] = (num / tot).astype(o_ref.dtype)

    for pi, (n_chunks, n_classes, _, _) in enumerate(patterns):
        class_rows = seq // (n_chunks * n_classes)
        blocks_per_class = class_rows // (BLOCK // n_chunks)
        shift = blocks_per_class.bit_length() - 1
        assert blocks_per_class == 1 << shift and n_classes * blocks_per_class == seq // BLOCK
        if multi and pi == len(patterns) - 1:
            assert n_chunks == 1
            for cls in range(n_classes):
                for n in range(blocks_per_class):
                    block_task(pi, cls, n)
                combine(cls * class_rows, class_rows)
        else:
            def task(t, carry, pi=pi, shift=shift, blocks_per_class=blocks_per_class):
                block_task(pi, lax.shift_right_logical(t, shift),
                           lax.bitwise_and(t, blocks_per_class - 1))
                return carry

            lax.fori_loop(0, seq // BLOCK, task, 0, unroll=True)


def _attention(q_arr, kv_arr, slopes, sinks, *, q_block, k_block, v_block, patterns, use_sinks):
    b, seq, _ = q_arr.shape
    n_blocks = N_HEADS // HEADS_PER_BLOCK
    n_pat = len(patterns)
    scratch = [pltpu.VMEM((n_pat * 4, BLOCK, 2 * BLOCK), F32)]
    if n_pat > 1:
        scratch += [pltpu.VMEM((n_pat, seq, LANES), F32) for _ in range(3)]
    smem = pl.BlockSpec(memory_space=pltpu.SMEM)
    return pl.pallas_call(
        functools.partial(_attn_kernel, patterns=patterns, use_sinks=use_sinks),
        grid=(n_blocks, b),
        in_specs=[
            smem, smem,
            pl.BlockSpec((None, seq, LANES), lambda j, i: (i, 0, q_block(j))),
            pl.BlockSpec((None, seq, LANES), lambda j, i: (i, 0, k_block(j))),
            pl.BlockSpec((None, seq, LANES), lambda j, i: (i, 0, v_block(j))),
        ],
        out_specs=pl.BlockSpec((None, seq, LANES), lambda j, i: (i, 0, j)),
        out_shape=jax.ShapeDtypeStruct((b, seq, D_MODEL), BF16),
        scratch_shapes=scratch,
        compiler_params=pltpu.CompilerParams(
            dimension_semantics=("arbitrary", "arbitrary"), vmem_limit_bytes=VMEM_LIMIT),
        name="attn_dilated" if n_pat > 1 else "attn_swa",
    )(slopes, sinks, q_arr, kv_arr, kv_arr)


def _alibi_slopes(n):
    return np.array([2.0 ** (-8.0 * (h + 1) / n) for h in range(n)], dtype=np.float32)


def _swa_head_order():
    order = []
    for c in range(N_KV_B // HEADS_PER_BLOCK):
        for g in range(GROUP_B):
            order += [(HEADS_PER_BLOCK * c) * GROUP_B + g, (HEADS_PER_BLOCK * c + 1) * GROUP_B + g]
    return np.array(order, dtype=np.int32)


def kernel(x, ffn1_w_in, ffn1_w_out, ffn2_w_in, ffn2_w_out, ln_g, ln_b,
           a_w_qkv, a_w_o, kv_w, b_w_q, b_sinks, b_w_o):
    b, seq, d = x.shape
    t = b * seq
    xs = x.reshape(t, d)
    slopes = _alibi_slopes(N_HEADS)
    no_sinks = jnp.zeros((N_HEADS,), F32)
    dil_patterns = tuple((MAX_DILATION // dil, dil, win // dil, float(dil))
                         for win, dil in DILATED_PATTERNS)
    swa_patterns = ((1, 1, WINDOW_B - 1, 1.0),)
    n_blocks = N_HEADS // HEADS_PER_BLOCK
    head_order = _swa_head_order()
    col_order = (head_order[:, None] * HEAD_DIM + np.arange(HEAD_DIM)[None, :]).reshape(-1)
    kv_blocks = N_KV_B // HEADS_PER_BLOCK
    kv = None
    order = "nat"
    for i in range(DEPTH):
        dilated = i < N_A_LAYERS
        j = i - N_A_LAYERS
        w_in, w_out = (ffn1_w_in, i), (ffn1_w_out, i)
        if dilated:
            xs = _block(xs, w_in, w_out, ln_g[i, 0], ln_b[i, 0], seq=seq,
                        in_order=order, out_order="cm")
            order = "cm"
            qkv = _proj(xs, (a_w_qkv, i), out_dtype=F32, n_scaled=d)
            qkv = qkv.reshape(b, seq, 3 * d)
            att = _attention(
                qkv, qkv, jnp.asarray(slopes), no_sinks,
                q_block=lambda jb: jb, k_block=lambda jb: n_blocks + jb,
                v_block=lambda jb: 2 * n_blocks + jb,
                patterns=dil_patterns, use_sinks=False)
            w_o = (a_w_o, i)
        else:
            xs, q = _block(xs, w_in, w_out, ln_g[i, 0], ln_b[i, 0], seq=seq,
                           in_order=order, out_order="nat",
                           proj=((b_w_q[j][:, col_order], None), Q_SCALE, BF16))
            order = "nat"
            att = _attention(
                q.reshape(b, seq, d), kv, jnp.asarray(slopes[head_order]),
                b_sinks[j][head_order],
                q_block=lambda jb: jb, k_block=lambda jb: jb // GROUP_B,
                v_block=lambda jb: kv_blocks + jb // GROUP_B,
                patterns=swa_patterns, use_sinks=True)
            w_o = (b_w_o[j][col_order, :], None)
        share_kv = i == N_A_LAYERS - 1
        out = _block(xs, (ffn2_w_in, i), (ffn2_w_out, i),
                     ln_g[i, 2], ln_b[i, 2], seq=seq, in_order=order, out_order="nat",
                     attn=(att.reshape(t, d), w_o, ln_g[i, 1], ln_b[i, 1]),
                     proj=((kv_w, None), None, BF16) if share_kv else None)
        order = "nat"
        if share_kv:
            xs, kv = out
            kv = kv.reshape(b, seq, 2 * N_KV_B * HEAD_DIM)
        else:
            xs = out
    return xs.reshape(b, seq, d)
```

```python
import functools
import math

import numpy as np
import jax
import jax.numpy as jnp
from jax import lax
from jax.experimental import pallas as pl
from jax.experimental.pallas import tpu as pltpu

D_MODEL = 1024
DEPTH = 2
HEAD_DIM = 64
N_HEADS = D_MODEL // HEAD_DIM
N_KV_B = 4
GROUP_B = N_HEADS // N_KV_B
D_FF = 2816
DILATED_PATTERNS = ((128, 1), (512, 4), (2048, 16))
MAX_DILATION = max(d for _, d in DILATED_PATTERNS)
WINDOW_B = 128
BLOCK = 128
N_A_LAYERS = DEPTH // 2
ALPHA = (2.0 * DEPTH) ** 0.25
LN_EPS = 1e-5

LANES = 128
HEADS_PER_BLOCK = LANES // HEAD_DIM
MASK_VALUE = -1e30
LOG2E = math.log2(math.e)
Q_SCALE = HEAD_DIM ** -0.5 * LOG2E
VMEM_LIMIT = 56 * 1024 * 1024
ROW_TILE = 512

BF16 = jnp.bfloat16
F32 = jnp.float32


def _layer_norm(z, g, b):
    mu = jnp.mean(z, axis=-1, keepdims=True)
    zc = z - mu
    var = jnp.mean(zc * zc, axis=-1, keepdims=True)
    return zc * lax.rsqrt(var + LN_EPS) * g + b


def _resident(shape):
    return pl.BlockSpec(shape, lambda *_: (0,) * len(shape), pipeline_mode=pl.Buffered(1))


CLASS_ROWS = ROW_TILE // MAX_DILATION
N_SLABS = D_MODEL // LANES
W_STEPS = 16


def _tile_spec(order, seq, w, n_tiles, lag=0):
    tile = lambda i: jnp.clip(i - W_STEPS - lag, 0, n_tiles - 1)
    if order == "nat":
        return pl.BlockSpec((ROW_TILE, w), lambda i: (tile(i), 0))
    tiles_per_batch = seq // ROW_TILE

    def index(i):
        i = tile(i)
        return (i // tiles_per_batch, 0, i % tiles_per_batch, 0)

    return pl.BlockSpec((None, MAX_DILATION, CLASS_ROWS, w), index)


def _tile_view(x, order, seq):
    t, w = x.shape
    if order == "nat":
        return x
    return x.reshape(t // seq, MAX_DILATION, seq // MAX_DILATION, w)


def _swa_head(p):
    per = HEADS_PER_BLOCK * GROUP_B
    return (HEADS_PER_BLOCK * (p // per) + p % HEADS_PER_BLOCK) * GROUP_B + (p % per) // HEADS_PER_BLOCK


def _weight_chunk_spec(w, layer, layout):
    rows, cols = w.shape[-2] // W_STEPS, w.shape[-1]
    chunk = lambda i: jnp.minimum(i, W_STEPS - 1)
    if layout == "head_rows":
        assert rows == HEAD_DIM and W_STEPS == N_HEADS
        chunk = lambda i: _swa_head(jnp.minimum(i, W_STEPS - 1))
    if layer is None:
        return pl.BlockSpec((rows, cols), lambda i: (chunk(i), 0))
    return pl.BlockSpec((None, rows, cols), lambda i: (layer, chunk(i), 0))


def _weight_scratch(w):
    return pltpu.VMEM(w.shape[-2:], BF16)


def _load_weight_chunks(step, entries):
    for src_ref, dst_ref, layout in entries:
        rows, cols = src_ref.shape
        dst_rows = pl.ds(pl.multiple_of(step * rows, rows), rows)
        if layout == "gate_up":
            half = cols // 2
            for k in range(half // LANES):
                lanes = slice(k * LANES, (k + 1) * LANES)
                dst_ref[dst_rows, 2 * k * LANES:(2 * k + 1) * LANES] = (
                    src_ref[:, lanes].astype(BF16))
                dst_ref[dst_rows, (2 * k + 1) * LANES:(2 * k + 2) * LANES] = (
                    src_ref[:, half + k * LANES:half + (k + 1) * LANES].astype(BF16))
        elif layout == "head_cols":
            for blk in range(cols // LANES):
                heads = [_swa_head(HEADS_PER_BLOCK * blk + e) for e in range(HEADS_PER_BLOCK)]
                dst_ref[dst_rows, blk * LANES:(blk + 1) * LANES] = jnp.concatenate(
                    [src_ref[:, h * HEAD_DIM:(h + 1) * HEAD_DIM] for h in heads],
                    axis=1).astype(BF16)
        else:
            dst_ref[dst_rows, :] = src_ref[...].astype(BF16)


SUB = 4
SUB_ROWS = ROW_TILE // SUB
assert SUB * SUB == MAX_DILATION


def _block_kernel(*refs, has_attn, has_proj, proj_scale, proj_layout, in_order, out_order):
    refs = list(refs)
    x_ref = refs.pop(0)
    if has_attn:
        a_ref, wo_src, g1_ref, b1_ref = refs[:4]
        del refs[:4]
    win_src, wout_src, g2_ref, b2_ref = refs[:4]
    del refs[:4]
    wp_src = refs.pop(0) if has_proj else None
    o_ref = refs.pop(0)
    p_ref = refs.pop(0) if has_proj else None
    win_ref, wout_ref = refs[:2]
    del refs[:2]
    weights = [(win_src, win_ref, "gate_up"), (wout_src, wout_ref, None)]
    if has_attn:
        wo_ref = refs.pop(0)
        weights.append((wo_src, wo_ref, None))
    if has_proj:
        wp_ref = refs.pop(0)
        weights.append((wp_src, wp_ref, proj_layout))
    slab_ref = refs.pop(0) if in_order != out_order else None
    z_ref = refs.pop(0)
    step = pl.program_id(0)
    last = pl.num_programs(0) - 1

    @pl.when(step < W_STEPS)
    def _():
        _load_weight_chunks(step, weights)

    @pl.when(step == 0)
    def _():
        z_ref[...] = jnp.zeros_like(z_ref)

    def compute():
        d = x_ref.shape[-1]
        x = x_ref[...].reshape(ROW_TILE, d)
        if has_attn:
            a = a_ref[...].reshape(ROW_TILE, a_ref.shape[-1])
            mix = jnp.dot(a.astype(BF16), wo_ref[...], preferred_element_type=F32)
            x = _layer_norm(ALPHA * x + mix, g1_ref[...], b1_ref[...])
        h = jnp.dot(x.astype(BF16), win_ref[...], preferred_element_type=F32)
        acts = []
        for k in range(D_FF // LANES):
            gate = h[:, 2 * k * LANES:(2 * k + 1) * LANES]
            up = h[:, (2 * k + 1) * LANES:(2 * k + 2) * LANES]
            acts.append((gate * jax.nn.sigmoid(gate) * up).astype(BF16))
        act = jnp.concatenate(acts, axis=1)
        ffn = jnp.dot(act, wout_ref[...], preferred_element_type=F32)
        z_ref[...] = ALPHA * x + 0.5 * ffn

    def finish():
        y = _layer_norm(z_ref[...], g2_ref[...], b2_ref[...])
        if in_order == out_order:
            o_ref[...] = y.reshape(o_ref.shape)
        elif out_order == "cm":
            for j in range(N_SLABS):
                slab_ref[0, j] = y[:, j * LANES:(j + 1) * LANES]
            for j in range(N_SLABS):
                for r1 in range(SUB):
                    slab_ref[1, j, r1 * SUB_ROWS:(r1 + 1) * SUB_ROWS, :] = (
                        slab_ref[0, j, pl.ds(r1, SUB_ROWS, stride=SUB), :])
            for r1 in range(SUB):
                for r2 in range(SUB):
                    o_ref[SUB * r2 + r1] = jnp.concatenate(
                        [slab_ref[1, j, pl.ds(r1 * SUB_ROWS + r2, CLASS_ROWS, stride=SUB), :]
                         for j in range(N_SLABS)], axis=1)
        else:
            for r1 in range(SUB):
                for r2 in range(SUB):
                    r = SUB * r2 + r1
                    for j in range(N_SLABS):
                        slab_ref[1, j, pl.ds(r1 * SUB_ROWS + r2, CLASS_ROWS, stride=SUB), :] = (
                            y[r * CLASS_ROWS:(r + 1) * CLASS_ROWS, j * LANES:(j + 1) * LANES])
            for j in range(N_SLABS):
                for r1 in range(SUB):
                    slab_ref[0, j, pl.ds(r1, SUB_ROWS, stride=SUB), :] = (
                        slab_ref[1, j, r1 * SUB_ROWS:(r1 + 1) * SUB_ROWS, :])
            y = jnp.concatenate([slab_ref[0, j] for j in range(N_SLABS)], axis=1)
            o_ref[...] = y
        if has_proj:
            assert out_order == "nat"
            res = jnp.dot(y.astype(BF16), wp_ref[...], preferred_element_type=F32)
            if proj_scale is not None:
                res = res * proj_scale
            p_ref[...] = res.astype(p_ref.dtype)

    @pl.when((step >= W_STEPS) & (step < last))
    def _():
        finish()
        compute()

    @pl.when(step == last)
    def _():
        finish()


def _block(x, w_in, w_out, g, b, *, seq, in_order, out_order, attn=None, proj=None):
    t, d = x.shape
    n_tiles = t // ROW_TILE
    vec = lambda v: v.reshape(1, d)
    in_spec = lambda w: _tile_spec(in_order, seq, w, n_tiles)
    out_spec = lambda order, w: _tile_spec(order, seq, w, n_tiles, lag=1)
    args, specs = [_tile_view(x, in_order, seq)], [in_spec(d)]
    scratch = [_weight_scratch(w_in[0]), _weight_scratch(w_out[0])]
    if attn is not None:
        att, w_o, g1, b1 = attn
        args += [_tile_view(att, in_order, seq), w_o[0], vec(g1), vec(b1)]
        specs += [in_spec(d), _weight_chunk_spec(*w_o), _resident((1, d)), _resident((1, d))]
        scratch.append(_weight_scratch(w_o[0]))
    args += [w_in[0], w_out[0], vec(g), vec(b)]
    specs += [_weight_chunk_spec(*w_in), _weight_chunk_spec(*w_out),
              _resident((1, d)), _resident((1, d))]
    out_shapes = [jax.ShapeDtypeStruct(_tile_view(x, out_order, seq).shape, F32)]
    out_specs = [out_spec(out_order, d)]
    proj_scale = proj_layout = None
    if proj is not None:
        w_p, proj_scale, p_dtype = proj
        proj_layout = w_p[2]
        n_p = w_p[0].shape[-1]
        args.append(w_p[0])
        specs.append(_weight_chunk_spec(*w_p))
        scratch.append(_weight_scratch(w_p[0]))
        out_shapes.append(jax.ShapeDtypeStruct((t, n_p), p_dtype))
        out_specs.append(out_spec("nat", n_p))
    if in_order != out_order:
        scratch.append(pltpu.VMEM((2, N_SLABS, ROW_TILE, LANES), F32))
    scratch.append(pltpu.VMEM((ROW_TILE, d), F32))
    outs = pl.pallas_call(
        functools.partial(_block_kernel, has_attn=attn is not None, has_proj=proj is not None,
                          proj_scale=proj_scale, proj_layout=proj_layout,
                          in_order=in_order, out_order=out_order),
        grid=(W_STEPS + n_tiles + 1,),
        in_specs=specs,
        out_specs=out_specs,
        out_shape=out_shapes,
        scratch_shapes=scratch,
        compiler_params=pltpu.CompilerParams(
            dimension_semantics=("arbitrary",), vmem_limit_bytes=VMEM_LIMIT),
        name="block_attn" if attn is not None else "block_ffn",
    )(*args)
    y = outs[0].reshape(t, d)
    return (y, outs[1]) if proj is not None else y


def _proj_kernel(x_ref, w_src, o_ref, w_ref, *, n_scaled):
    step = pl.program_id(0)

    @pl.when(step < W_STEPS)
    def _():
        _load_weight_chunks(step, [(w_src, w_ref, None)])

    @pl.when(step >= W_STEPS)
    def _():
        res = jnp.dot(x_ref[...].astype(BF16), w_ref[...], preferred_element_type=F32)
        if n_scaled:
            col = lax.broadcasted_iota(jnp.int32, (1, res.shape[1]), 1)
            res = res * jnp.where(col < n_scaled, Q_SCALE, 1.0)
        o_ref[...] = res.astype(o_ref.dtype)


def _proj(x, w, *, out_dtype, n_scaled=0):
    t, d = x.shape
    n = w[0].shape[-1]
    n_tiles = t // ROW_TILE
    return pl.pallas_call(
        functools.partial(_proj_kernel, n_scaled=n_scaled),
        grid=(W_STEPS + n_tiles,),
        in_specs=[_tile_spec("nat", 0, d, n_tiles), _weight_chunk_spec(*w)],
        out_specs=_tile_spec("nat", 0, n, n_tiles),
        out_shape=jax.ShapeDtypeStruct((t, n), out_dtype),
        scratch_shapes=[_weight_scratch(w[0])],
        compiler_params=pltpu.CompilerParams(
            dimension_semantics=("arbitrary",), vmem_limit_bytes=VMEM_LIMIT),
        name="proj",
    )(x, w[0])


def _attn_kernel(slope_ref, sink_ref, q_ref, k_ref, v_ref, o_ref, bias_ref, *scratch,
                 patterns, use_sinks):
    seq = q_ref.shape[0]
    blk = pl.program_id(0)
    multi = len(patterns) > 1
    if multi:
        acc_ref, max_ref, den_ref = scratch

    lane = lax.broadcasted_iota(jnp.int32, (BLOCK, LANES), 1)
    low_half = lane < HEAD_DIM
    head_sel = [jnp.where((lane >= h * HEAD_DIM) & (lane < (h + 1) * HEAD_DIM), 1.0, 0.0).astype(BF16)
                for h in range(HEADS_PER_BLOCK)]
    sel2 = [jnp.concatenate([sel, sel], axis=0) for sel in head_sel]

    @pl.when(pl.program_id(1) == 0)
    def _():
        row = lax.broadcasted_iota(jnp.int32, (BLOCK, 2 * BLOCK), 0)
        col = lax.broadcasted_iota(jnp.int32, (BLOCK, 2 * BLOCK), 1)
        in_cur = col >= BLOCK
        for pi, (n_chunks, _, max_dist, scale) in enumerate(patterns):
            chunk_len = BLOCK // n_chunks
            sh = chunk_len.bit_length() - 1
            pos = lambda x: n_chunks * (x & (chunk_len - 1)) + (x >> sh)
            dist = pos(row) - pos(col & (BLOCK - 1)) + jnp.where(in_cur, 0, BLOCK)
            valid = (dist >= 0) & (dist <= max_dist)
            for h in range(HEADS_PER_BLOCK):
                slope = slope_ref[blk * HEADS_PER_BLOCK + h]
                bias = jnp.where(valid, (-slope * scale * LOG2E) * dist.astype(F32), MASK_VALUE)
                bias_ref[(pi * 2 + h) * 2 + 0] = bias
                bias_ref[(pi * 2 + h) * 2 + 1] = jnp.where(in_cur, bias, MASK_VALUE)

    if use_sinks:
        sink2 = jnp.where(low_half, sink_ref[blk * HEADS_PER_BLOCK] * LOG2E,
                          sink_ref[blk * HEADS_PER_BLOCK + 1] * LOG2E)

    def block_task(pi, cls, n):
        n_chunks, n_classes, _, _ = patterns[pi]
        chunk_len = BLOCK // n_chunks
        class_rows = seq // (n_chunks * n_classes)
        static = isinstance(n, int)

        def chunks(n):
            starts = [(n_classes * c + cls) * class_rows + n * chunk_len for c in range(n_chunks)]
            return [pl.ds(s if static else pl.multiple_of(s, chunk_len), chunk_len)
                    for s in starts]

        def load(ref, slices):
            parts = [ref[s, :] for s in slices]
            return parts[0] if len(parts) == 1 else jnp.concatenate(parts, axis=0)

        def store(ref, slices, val):
            for c, s in enumerate(slices):
                part = val[c * chunk_len:(c + 1) * chunk_len]
                if multi:
                    ref[pi, s, :] = part
                else:
                    ref[s, :] = part.astype(ref.dtype)

        if static:
            first, prev_n = int(n == 0), max(n - 1, 0)
        else:
            first, prev_n = (n == 0).astype(jnp.int32), jnp.maximum(n - 1, 0)
        cur = chunks(n)
        prev = chunks(prev_n)
        q = load(q_ref, cur).astype(BF16)
        kk = jnp.concatenate([load(k_ref, prev), load(k_ref, cur)], axis=0).astype(BF16)
        vv = jnp.concatenate([load(v_ref, prev), load(v_ref, cur)], axis=0).astype(BF16)
        q2 = jnp.concatenate([q * head_sel[h] for h in range(HEADS_PER_BLOCK)], axis=0)
        s2 = lax.dot_general(q2, kk, (((1,), (1,)), ((), ())),
                             preferred_element_type=F32)
        ps, ms = [], []
        for h in range(HEADS_PER_BLOCK):
            s = s2[h * BLOCK:(h + 1) * BLOCK] + bias_ref[(pi * 2 + h) * 2 + first]
            m = jnp.max(s, axis=-1, keepdims=True)
            ps.append(jnp.exp2(s - m).astype(BF16))
            ms.append(m)
        rhs = jnp.concatenate(
            [jnp.concatenate([vv * sel2[h], sel2[h]], axis=1) for h in range(HEADS_PER_BLOCK)],
            axis=0)
        nd = jnp.dot(jnp.concatenate(ps, axis=1), rhs, preferred_element_type=F32)
        num, den = nd[:, :LANES], nd[:, LANES:]
        m2 = jnp.where(low_half, ms[0], ms[1])
        if multi:
            store(acc_ref, cur, num)
            store(max_ref, cur, m2)
            store(den_ref, cur, den)
        else:
            if use_sinks:
                den = den + jnp.exp2(sink2 - m2)
            store(o_ref, cur, num / den)

    def combine(row0, n_rows):
        rows = pl.ds(row0, n_rows)
        mxs = [max_ref[pi, rows, :] for pi in range(len(patterns))]
        mx = functools.reduce(jnp.maximum, mxs)
        es = [jnp.exp2(m - mx) for m in mxs]
        add = lambda a, b: a + b
        num = functools.reduce(add, [e * acc_ref[pi, rows, :] for pi, e in enumerate(es)])
        tot = functools.reduce(add, [e * den_ref[pi, rows, :] for pi, e in enumerate(es)])
        o_ref[rows, :] = (num / tot).astype(o_ref.dtype)

    for pi, (n_chunks, n_classes, _, _) in enumerate(patterns):
        class_rows = seq // (n_chunks * n_classes)
        blocks_per_class = class_rows // (BLOCK // n_chunks)
        shift = blocks_per_class.bit_length() - 1
        assert blocks_per_class == 1 << shift and n_classes * blocks_per_class == seq // BLOCK
        if multi and pi == len(patterns) - 1:
            assert n_chunks == 1
            for cls in range(n_classes):
                for n in range(blocks_per_class):
                    block_task(pi, cls, n)
                combine(cls * class_rows, class_rows)
        else:
            def task(t, carry, pi=pi, shift=shift, blocks_per_class=blocks_per_class):
                block_task(pi, lax.shift_right_logical(t, shift),
                           lax.bitwise_and(t, blocks_per_class - 1))
                return carry

            lax.fori_loop(0, seq // BLOCK, task, 0, unroll=True)


def _attention(q_arr, kv_arr, slopes, sinks, *, q_block, k_block, v_block, patterns, use_sinks):
    b, seq, _ = q_arr.shape
    n_blocks = N_HEADS // HEADS_PER_BLOCK
    n_pat = len(patterns)
    scratch = [pltpu.VMEM((n_pat * 4, BLOCK, 2 * BLOCK), F32)]
    if n_pat > 1:
        scratch += [pltpu.VMEM((n_pat, seq, LANES), F32) for _ in range(3)]
    smem = pl.BlockSpec(memory_space=pltpu.SMEM)
    return pl.pallas_call(
        functools.partial(_attn_kernel, patterns=patterns, use_sinks=use_sinks),
        grid=(n_blocks, b),
        in_specs=[
            smem, smem,
            pl.BlockSpec((None, seq, LANES), lambda j, i: (i, 0, q_block(j))),
            pl.BlockSpec((None, seq, LANES), lambda j, i: (i, 0, k_block(j))),
            pl.BlockSpec((None, seq, LANES), lambda j, i: (i, 0, v_block(j))),
        ],
        out_specs=pl.BlockSpec((None, seq, LANES), lambda j, i: (i, 0, j)),
        out_shape=jax.ShapeDtypeStruct((b, seq, D_MODEL), BF16),
        scratch_shapes=scratch,
        compiler_params=pltpu.CompilerParams(
            dimension_semantics=("arbitrary", "arbitrary"), vmem_limit_bytes=VMEM_LIMIT),
        name="attn_dilated" if n_pat > 1 else "attn_swa",
    )(slopes, sinks, q_arr, kv_arr, kv_arr)


def _alibi_slopes(n):
    return np.array([2.0 ** (-8.0 * (h + 1) / n) for h in range(n)], dtype=np.float32)


def kernel(x, ffn1_w_in, ffn1_w_out, ffn2_w_in, ffn2_w_out, ln_g, ln_b,
           a_w_qkv, a_w_o, kv_w, b_w_q, b_sinks, b_w_o):
    b, seq, d = x.shape
    t = b * seq
    xs = x.reshape(t, d)
    slopes = _alibi_slopes(N_HEADS)
    no_sinks = jnp.zeros((N_HEADS,), F32)
    dil_patterns = tuple((MAX_DILATION // dil, dil, win // dil, float(dil))
                         for win, dil in DILATED_PATTERNS)
    swa_patterns = ((1, 1, WINDOW_B - 1, 1.0),)
    n_blocks = N_HEADS // HEADS_PER_BLOCK
    head_order = np.array([_swa_head(p) for p in range(N_HEADS)], dtype=np.int32)
    kv_blocks = N_KV_B // HEADS_PER_BLOCK
    kv = None
    order = "nat"
    for i in range(DEPTH):
        dilated = i < N_A_LAYERS
        j = i - N_A_LAYERS
        w_in, w_out = (ffn1_w_in, i, "gate_up"), (ffn1_w_out, i, None)
        if dilated:
            xs = _block(xs, w_in, w_out, ln_g[i, 0], ln_b[i, 0], seq=seq,
                        in_order=order, out_order="cm")
            order = "cm"
            qkv = _proj(xs, (a_w_qkv, i, None), out_dtype=F32, n_scaled=d)
            qkv = qkv.reshape(b, seq, 3 * d)
            att = _attention(
                qkv, qkv, jnp.asarray(slopes), no_sinks,
                q_block=lambda jb: jb, k_block=lambda jb: n_blocks + jb,
                v_block=lambda jb: 2 * n_blocks + jb,
                patterns=dil_patterns, use_sinks=False)
            w_o = (a_w_o, i, None)
        else:
            xs, q = _block(xs, w_in, w_out, ln_g[i, 0], ln_b[i, 0], seq=seq,
                           in_order=order, out_order="nat",
                           proj=((b_w_q, j, "head_cols"), Q_SCALE, BF16))
            order = "nat"
            att = _attention(
                q.reshape(b, seq, d), kv, jnp.asarray(slopes[head_order]),
                b_sinks[j][head_order],
                q_block=lambda jb: jb, k_block=lambda jb: jb // GROUP_B,
                v_block=lambda jb: kv_blocks + jb // GROUP_B,
                patterns=swa_patterns, use_sinks=True)
            w_o = (b_w_o, j, "head_rows")
        share_kv = i == N_A_LAYERS - 1
        out = _block(xs, (ffn2_w_in, i, "gate_up"), (ffn2_w_out, i, None),
                     ln_g[i, 2], ln_b[i, 2], seq=seq, in_order=order, out_order="nat",
                     attn=(att.reshape(t, d), w_o, ln_g[i, 1], ln_b[i, 1]),
                     proj=((kv_w, None, None), None, BF16) if share_kv else None)
        order = "nat"
        if share_kv:
            xs, kv = out
            kv = kv.reshape(b, seq, 2 * N_KV_B * HEAD_DIM)
        else:
            xs = out
    return xs.reshape(b, seq, d)
```

```python
import functools
import math

import numpy as np
import jax
import jax.numpy as jnp
from jax import lax
from jax.experimental import pallas as pl
from jax.experimental.pallas import tpu as pltpu

D_MODEL = 1024
DEPTH = 2
HEAD_DIM = 64
N_HEADS = D_MODEL // HEAD_DIM
N_KV_B = 4
GROUP_B = N_HEADS // N_KV_B
D_FF = 2816
DILATED_PATTERNS = ((128, 1), (512, 4), (2048, 16))
MAX_DILATION = max(d for _, d in DILATED_PATTERNS)
WINDOW_B = 128
BLOCK = 128
N_A_LAYERS = DEPTH // 2
ALPHA = (2.0 * DEPTH) ** 0.25
LN_EPS = 1e-5

LANES = 128
HEADS_PER_BLOCK = LANES // HEAD_DIM
MASK_VALUE = -1e30
LOG2E = math.log2(math.e)
Q_SCALE = HEAD_DIM ** -0.5 * LOG2E
VMEM_LIMIT = 56 * 1024 * 1024
ROW_TILE = 512
PROJ_ROWS = 1024

BF16 = jnp.bfloat16
F32 = jnp.float32


def _layer_norm(z, g, b):
    mu = jnp.mean(z, axis=-1, keepdims=True)
    zc = z - mu
    var = jnp.mean(zc * zc, axis=-1, keepdims=True)
    return zc * lax.rsqrt(var + LN_EPS) * g + b


def _resident(shape):
    return pl.BlockSpec(shape, lambda *_: (0,) * len(shape), pipeline_mode=pl.Buffered(1))


CLASS_ROWS = ROW_TILE // MAX_DILATION
N_SLABS = D_MODEL // LANES
W_STEPS = 16


def _tile_spec(order, seq, w, n_tiles, lag=0, rows=ROW_TILE):
    tile = lambda i: jnp.clip(i - W_STEPS - lag, 0, n_tiles - 1)
    if order == "nat":
        return pl.BlockSpec((rows, w), lambda i: (tile(i), 0))
    assert rows == ROW_TILE
    tiles_per_batch = seq // ROW_TILE

    def index(i):
        i = tile(i)
        return (i // tiles_per_batch, 0, i % tiles_per_batch, 0)

    return pl.BlockSpec((None, MAX_DILATION, CLASS_ROWS, w), index)


def _tile_view(x, order, seq):
    t, w = x.shape
    if order == "nat":
        return x
    return x.reshape(t // seq, MAX_DILATION, seq // MAX_DILATION, w)


def _swa_head(p):
    per = HEADS_PER_BLOCK * GROUP_B
    return (HEADS_PER_BLOCK * (p // per) + p % HEADS_PER_BLOCK) * GROUP_B + (p % per) // HEADS_PER_BLOCK


def _weight_chunk_spec(w, layer, layout):
    rows, cols = w.shape[-2] // W_STEPS, w.shape[-1]
    chunk = lambda i: jnp.minimum(i, W_STEPS - 1)
    if layout == "head_rows":
        assert rows == HEAD_DIM and W_STEPS == N_HEADS
        chunk = lambda i: _swa_head(jnp.minimum(i, W_STEPS - 1))
    if layer is None:
        return pl.BlockSpec((rows, cols), lambda i: (chunk(i), 0))
    return pl.BlockSpec((None, rows, cols), lambda i: (layer, chunk(i), 0))


def _weight_scratch(w):
    return pltpu.VMEM(w.shape[-2:], BF16)


def _load_weight_chunks(step, entries):
    for src_ref, dst_ref, layout in entries:
        rows, cols = src_ref.shape
        dst_rows = pl.ds(pl.multiple_of(step * rows, rows), rows)
        if layout == "gate_up":
            half = cols // 2
            for k in range(half // LANES):
                lanes = slice(k * LANES, (k + 1) * LANES)
                dst_ref[dst_rows, 2 * k * LANES:(2 * k + 1) * LANES] = (
                    src_ref[:, lanes].astype(BF16))
                dst_ref[dst_rows, (2 * k + 1) * LANES:(2 * k + 2) * LANES] = (
                    src_ref[:, half + k * LANES:half + (k + 1) * LANES].astype(BF16))
        elif layout == "head_cols":
            for blk in range(cols // LANES):
                heads = [_swa_head(HEADS_PER_BLOCK * blk + e) for e in range(HEADS_PER_BLOCK)]
                dst_ref[dst_rows, blk * LANES:(blk + 1) * LANES] = jnp.concatenate(
                    [src_ref[:, h * HEAD_DIM:(h + 1) * HEAD_DIM] for h in heads],
                    axis=1).astype(BF16)
        else:
            dst_ref[dst_rows, :] = src_ref[...].astype(BF16)


SUB = 4
SUB_ROWS = ROW_TILE // SUB
assert SUB * SUB == MAX_DILATION


def _block_kernel(*refs, has_attn, has_proj, proj_scale, proj_layout, in_order, out_order):
    refs = list(refs)
    x_ref = refs.pop(0)
    if has_attn:
        a_ref, wo_src, g1_ref, b1_ref = refs[:4]
        del refs[:4]
    win_src, wout_src, g2_ref, b2_ref = refs[:4]
    del refs[:4]
    wp_src = refs.pop(0) if has_proj else None
    o_ref = refs.pop(0)
    p_ref = refs.pop(0) if has_proj else None
    win_ref, wout_ref = refs[:2]
    del refs[:2]
    weights = [(win_src, win_ref, "gate_up"), (wout_src, wout_ref, None)]
    if has_attn:
        wo_ref = refs.pop(0)
        weights.append((wo_src, wo_ref, None))
    if has_proj:
        wp_ref = refs.pop(0)
        weights.append((wp_src, wp_ref, proj_layout))
    slab_ref = refs.pop(0) if in_order != out_order else None
    z_ref = refs.pop(0)
    step = pl.program_id(0)
    last = pl.num_programs(0) - 1

    @pl.when(step < W_STEPS)
    def _():
        _load_weight_chunks(step, weights)

    @pl.when(step == 0)
    def _():
        z_ref[...] = jnp.zeros_like(z_ref)

    def compute():
        d = x_ref.shape[-1]
        x = x_ref[...].reshape(ROW_TILE, d)
        if has_attn:
            a = a_ref[...].reshape(ROW_TILE, a_ref.shape[-1])
            mix = jnp.dot(a.astype(BF16), wo_ref[...], preferred_element_type=F32)
            x = _layer_norm(ALPHA * x + mix, g1_ref[...], b1_ref[...])
        h = jnp.dot(x.astype(BF16), win_ref[...], preferred_element_type=F32)
        acts = []
        for k in range(D_FF // LANES):
            gate = h[:, 2 * k * LANES:(2 * k + 1) * LANES]
            up = h[:, (2 * k + 1) * LANES:(2 * k + 2) * LANES]
            acts.append((gate * jax.nn.sigmoid(gate) * up).astype(BF16))
        act = jnp.concatenate(acts, axis=1)
        ffn = jnp.dot(act, wout_ref[...], preferred_element_type=F32)
        z_ref[...] = ALPHA * x + 0.5 * ffn

    def finish():
        y = _layer_norm(z_ref[...], g2_ref[...], b2_ref[...])
        if in_order == out_order:
            o_ref[...] = y.reshape(o_ref.shape)
        elif out_order == "cm":
            for j in range(N_SLABS):
                slab_ref[0, j] = y[:, j * LANES:(j + 1) * LANES]
            for j in range(N_SLABS):
                for r1 in range(SUB):
                    slab_ref[1, j, r1 * SUB_ROWS:(r1 + 1) * SUB_ROWS, :] = (
                        slab_ref[0, j, pl.ds(r1, SUB_ROWS, stride=SUB), :])
            for r1 in range(SUB):
                for r2 in range(SUB):
                    o_ref[SUB * r2 + r1] = jnp.concatenate(
                        [slab_ref[1, j, pl.ds(r1 * SUB_ROWS + r2, CLASS_ROWS, stride=SUB), :]
                         for j in range(N_SLABS)], axis=1)
        else:
            for r1 in range(SUB):
                for r2 in range(SUB):
                    r = SUB * r2 + r1
                    for j in range(N_SLABS):
                        slab_ref[1, j, pl.ds(r1 * SUB_ROWS + r2, CLASS_ROWS, stride=SUB), :] = (
                            y[r * CLASS_ROWS:(r + 1) * CLASS_ROWS, j * LANES:(j + 1) * LANES])
            for j in range(N_SLABS):
                for r1 in range(SUB):
                    slab_ref[0, j, pl.ds(r1, SUB_ROWS, stride=SUB), :] = (
                        slab_ref[1, j, r1 * SUB_ROWS:(r1 + 1) * SUB_ROWS, :])
            y = jnp.concatenate([slab_ref[0, j] for j in range(N_SLABS)], axis=1)
            o_ref[...] = y
        if has_proj:
            assert out_order == "nat"
            res = jnp.dot(y.astype(BF16), wp_ref[...], preferred_element_type=F32)
            if proj_scale is not None:
                res = res * proj_scale
            p_ref[...] = res.astype(p_ref.dtype)

    @pl.when((step >= W_STEPS) & (step < last))
    def _():
        finish()
        compute()

    @pl.when(step == last)
    def _():
        finish()


def _block(x, w_in, w_out, g, b, *, seq, in_order, out_order, attn=None, proj=None):
    t, d = x.shape
    n_tiles = t // ROW_TILE
    vec = lambda v: v.reshape(1, d)
    in_spec = lambda w: _tile_spec(in_order, seq, w, n_tiles)
    out_spec = lambda order, w: _tile_spec(order, seq, w, n_tiles, lag=1)
    args, specs = [_tile_view(x, in_order, seq)], [in_spec(d)]
    scratch = [_weight_scratch(w_in[0]), _weight_scratch(w_out[0])]
    if attn is not None:
        att, w_o, g1, b1 = attn
        args += [_tile_view(att, in_order, seq), w_o[0], vec(g1), vec(b1)]
        specs += [in_spec(d), _weight_chunk_spec(*w_o), _resident((1, d)), _resident((1, d))]
        scratch.append(_weight_scratch(w_o[0]))
    args += [w_in[0], w_out[0], vec(g), vec(b)]
    specs += [_weight_chunk_spec(*w_in), _weight_chunk_spec(*w_out),
              _resident((1, d)), _resident((1, d))]
    out_shapes = [jax.ShapeDtypeStruct(_tile_view(x, out_order, seq).shape, F32)]
    out_specs = [out_spec(out_order, d)]
    proj_scale = proj_layout = None
    if proj is not None:
        w_p, proj_scale, p_dtype = proj
        proj_layout = w_p[2]
        n_p = w_p[0].shape[-1]
        args.append(w_p[0])
        specs.append(_weight_chunk_spec(*w_p))
        scratch.append(_weight_scratch(w_p[0]))
        out_shapes.append(jax.ShapeDtypeStruct((t, n_p), p_dtype))
        out_specs.append(out_spec("nat", n_p))
    if in_order != out_order:
        scratch.append(pltpu.VMEM((2, N_SLABS, ROW_TILE, LANES), F32))
    scratch.append(pltpu.VMEM((ROW_TILE, d), F32))
    outs = pl.pallas_call(
        functools.partial(_block_kernel, has_attn=attn is not None, has_proj=proj is not None,
                          proj_scale=proj_scale, proj_layout=proj_layout,
                          in_order=in_order, out_order=out_order),
        grid=(W_STEPS + n_tiles + 1,),
        in_specs=specs,
        out_specs=out_specs,
        out_shape=out_shapes,
        scratch_shapes=scratch,
        compiler_params=pltpu.CompilerParams(
            dimension_semantics=("arbitrary",), vmem_limit_bytes=VMEM_LIMIT),
        name="block_attn" if attn is not None else "block_ffn",
    )(*args)
    y = outs[0].reshape(t, d)
    return (y, outs[1]) if proj is not None else y


def _proj_kernel(x_ref, w_src, o_ref, w_ref, *, n_scaled):
    step = pl.program_id(0)

    @pl.when(step < W_STEPS)
    def _():
        _load_weight_chunks(step, [(w_src, w_ref, None)])

    @pl.when(step >= W_STEPS)
    def _():
        res = jnp.dot(x_ref[...].astype(BF16), w_ref[...], preferred_element_type=F32)
        if n_scaled:
            col = lax.broadcasted_iota(jnp.int32, (1, res.shape[1]), 1)
            res = res * jnp.where(col < n_scaled, Q_SCALE, 1.0)
        o_ref[...] = res.astype(o_ref.dtype)


def _proj(x, w, *, out_dtype, n_scaled=0):
    t, d = x.shape
    n = w[0].shape[-1]
    n_tiles = t // PROJ_ROWS
    return pl.pallas_call(
        functools.partial(_proj_kernel, n_scaled=n_scaled),
        grid=(W_STEPS + n_tiles,),
        in_specs=[_tile_spec("nat", 0, d, n_tiles, rows=PROJ_ROWS), _weight_chunk_spec(*w)],
        out_specs=_tile_spec("nat", 0, n, n_tiles, rows=PROJ_ROWS),
        out_shape=jax.ShapeDtypeStruct((t, n), out_dtype),
        scratch_shapes=[_weight_scratch(w[0])],
        compiler_params=pltpu.CompilerParams(
            dimension_semantics=("arbitrary",), vmem_limit_bytes=VMEM_LIMIT),
        name="proj",
    )(x, w[0])


def _attn_kernel(slope_ref, sink_ref, q_ref, k_ref, v_ref, o_ref, bias_ref, *scratch,
                 patterns, use_sinks):
    seq = q_ref.shape[0]
    blk = pl.program_id(0)
    multi = len(patterns) > 1
    if multi:
        acc_ref, max_ref, den_ref = scratch

    lane = lax.broadcasted_iota(jnp.int32, (BLOCK, LANES), 1)
    low_half = lane < HEAD_DIM
    head_sel = [jnp.where((lane >= h * HEAD_DIM) & (lane < (h + 1) * HEAD_DIM), 1.0, 0.0).astype(BF16)
                for h in range(HEADS_PER_BLOCK)]
    sel2 = [jnp.concatenate([sel, sel], axis=0) for sel in head_sel]

    @pl.when(pl.program_id(1) == 0)
    def _():
        row = lax.broadcasted_iota(jnp.int32, (BLOCK, 2 * BLOCK), 0)
        col = lax.broadcasted_iota(jnp.int32, (BLOCK, 2 * BLOCK), 1)
        in_cur = col >= BLOCK
        for pi, (n_chunks, _, max_dist, scale) in enumerate(patterns):
            chunk_len = BLOCK // n_chunks
            sh = chunk_len.bit_length() - 1
            pos = lambda x: n_chunks * (x & (chunk_len - 1)) + (x >> sh)
            dist = pos(row) - pos(col & (BLOCK - 1)) + jnp.where(in_cur, 0, BLOCK)
            valid = (dist >= 0) & (dist <= max_dist)
            for h in range(HEADS_PER_BLOCK):
                slope = slope_ref[blk * HEADS_PER_BLOCK + h]
                bias = jnp.where(valid, (-slope * scale * LOG2E) * dist.astype(F32), MASK_VALUE)
                bias_ref[(pi * 2 + h) * 2 + 0] = bias
                bias_ref[(pi * 2 + h) * 2 + 1] = jnp.where(in_cur, bias, MASK_VALUE)

    if use_sinks:
        sink2 = jnp.where(low_half, sink_ref[blk * HEADS_PER_BLOCK] * LOG2E,
                          sink_ref[blk * HEADS_PER_BLOCK + 1] * LOG2E)

    def block_task(pi, cls, n):
        n_chunks, n_classes, _, _ = patterns[pi]
        chunk_len = BLOCK // n_chunks
        class_rows = seq // (n_chunks * n_classes)
        static = isinstance(n, int)

        def chunks(n):
            starts = [(n_classes * c + cls) * class_rows + n * chunk_len for c in range(n_chunks)]
            return [pl.ds(s if static else pl.multiple_of(s, chunk_len), chunk_len)
                    for s in starts]

        def load(ref, slices):
            parts = [ref[s, :] for s in slices]
            return parts[0] if len(parts) == 1 else jnp.concatenate(parts, axis=0)

        def store(ref, slices, val):
            for c, s in enumerate(slices):
                part = val[c * chunk_len:(c + 1) * chunk_len]
                if multi:
                    ref[pi, s, :] = part
                else:
                    ref[s, :] = part.astype(ref.dtype)

        if static:
            first, prev_n = int(n == 0), max(n - 1, 0)
        else:
            first, prev_n = (n == 0).astype(jnp.int32), jnp.maximum(n - 1, 0)
        cur = chunks(n)
        prev = chunks(prev_n)
        q = load(q_ref, cur).astype(BF16)
        kk = jnp.concatenate([load(k_ref, prev), load(k_ref, cur)], axis=0).astype(BF16)
        vv = jnp.concatenate([load(v_ref, prev), load(v_ref, cur)], axis=0).astype(BF16)
        q2 = jnp.concatenate([q * head_sel[h] for h in range(HEADS_PER_BLOCK)], axis=0)
        s2 = lax.dot_general(q2, kk, (((1,), (1,)), ((), ())),
                             preferred_element_type=F32)
        ps, ms = [], []
        for h in range(HEADS_PER_BLOCK):
            s = s2[h * BLOCK:(h + 1) * BLOCK] + bias_ref[(pi * 2 + h) * 2 + first]
            m = jnp.max(s, axis=-1, keepdims=True)
            ps.append(jnp.exp2(s - m).astype(BF16))
            ms.append(m)
        rhs = jnp.concatenate(
            [jnp.concatenate([vv * sel2[h], sel2[h]], axis=1) for h in range(HEADS_PER_BLOCK)],
            axis=0)
        nd = jnp.dot(jnp.concatenate(ps, axis=1), rhs, preferred_element_type=F32)
        num, den = nd[:, :LANES], nd[:, LANES:]
        m2 = jnp.where(low_half, ms[0], ms[1])
        if multi:
            store(acc_ref, cur, num)
            store(max_ref, cur, m2)
            store(den_ref, cur, den)
        else:
            if use_sinks:
                den = den + jnp.exp2(sink2 - m2)
            store(o_ref, cur, num / den)

    def combine(row0, n_rows):
        rows = pl.ds(row0, n_rows)
        mxs = [max_ref[pi, rows, :] for pi in range(len(patterns))]
        mx = functools.reduce(jnp.maximum, mxs)
        es = [jnp.exp2(m - mx) for m in mxs]
        add = lambda a, b: a + b
        num = functools.reduce(add, [e * acc_ref[pi, rows, :] for pi, e in enumerate(es)])
        tot = functools.reduce(add, [e * den_ref[pi, rows, :] for pi, e in enumerate(es)])
        o_ref[rows, :] = (num / tot).astype(o_ref.dtype)

    for pi, (n_chunks, n_classes, _, _) in enumerate(patterns):
        class_rows = seq // (n_chunks * n_classes)
        blocks_per_class = class_rows // (BLOCK // n_chunks)
        shift = blocks_per_class.bit_length() - 1
        assert blocks_per_class == 1 << shift and n_classes * blocks_per_class == seq // BLOCK
        if multi and pi == len(patterns) - 1:
            assert n_chunks == 1
            for cls in range(n_classes):
                for n in range(blocks_per_class):
                    block_task(pi, cls, n)
                combine(cls * class_rows, class_rows)
        else:
            def task(t, carry, pi=pi, shift=shift, blocks_per_class=blocks_per_class):
                block_task(pi, lax.shift_right_logical(t, shift),
                           lax.bitwise_and(t, blocks_per_class - 1))
                return carry

            lax.fori_loop(0, seq // BLOCK, task, 0, unroll=True)


def _attention(q_arr, kv_arr, slopes, sinks, *, q_block, k_block, v_block, patterns, use_sinks):
    b, seq, _ = q_arr.shape
    n_blocks = N_HEADS // HEADS_PER_BLOCK
    n_pat = len(patterns)
    scratch = [pltpu.VMEM((n_pat * 4, BLOCK, 2 * BLOCK), F32)]
    if n_pat > 1:
        scratch += [pltpu.VMEM((n_pat, seq, LANES), F32) for _ in range(3)]
    smem = pl.BlockSpec(memory_space=pltpu.SMEM)
    return pl.pallas_call(
        functools.partial(_attn_kernel, patterns=patterns, use_sinks=use_sinks),
        grid=(n_blocks, b),
        in_specs=[
            smem, smem,
            pl.BlockSpec((None, seq, LANES), lambda j, i: (i, 0, q_block(j))),
            pl.BlockSpec((None, seq, LANES), lambda j, i: (i, 0, k_block(j))),
            pl.BlockSpec((None, seq, LANES), lambda j, i: (i, 0, v_block(j))),
        ],
        out_specs=pl.BlockSpec((None, seq, LANES), lambda j, i: (i, 0, j)),
        out_shape=jax.ShapeDtypeStruct((b, seq, D_MODEL), BF16),
        scratch_shapes=scratch,
        compiler_params=pltpu.CompilerParams(
            dimension_semantics=("arbitrary", "arbitrary"), vmem_limit_bytes=VMEM_LIMIT),
        name="attn_dilated" if n_pat > 1 else "attn_swa",
    )(slopes, sinks, q_arr, kv_arr, kv_arr)


def _alibi_slopes(n):
    return np.array([2.0 ** (-8.0 * (h + 1) / n) for h in range(n)], dtype=np.float32)


def kernel(x, ffn1_w_in, ffn1_w_out, ffn2_w_in, ffn2_w_out, ln_g, ln_b,
           a_w_qkv, a_w_o, kv_w, b_w_q, b_sinks, b_w_o):
    b, seq, d = x.shape
    t = b * seq
    xs = x.reshape(t, d)
    slopes = _alibi_slopes(N_HEADS)
    no_sinks = jnp.zeros((N_HEADS,), F32)
    dil_patterns = tuple((MAX_DILATION // dil, dil, win // dil, float(dil))
                         for win, dil in DILATED_PATTERNS)
    swa_patterns = ((1, 1, WINDOW_B - 1, 1.0),)
    n_blocks = N_HEADS // HEADS_PER_BLOCK
    head_order = np.array([_swa_head(p) for p in range(N_HEADS)], dtype=np.int32)
    kv_blocks = N_KV_B // HEADS_PER_BLOCK
    kv = None
    order = "nat"
    for i in range(DEPTH):
        dilated = i < N_A_LAYERS
        j = i - N_A_LAYERS
        w_in, w_out = (ffn1_w_in, i, "gate_up"), (ffn1_w_out, i, None)
        if dilated:
            xs = _block(xs, w_in, w_out, ln_g[i, 0], ln_b[i, 0], seq=seq,
                        in_order=order, out_order="cm")
            order = "cm"
            qkv = _proj(xs, (a_w_qkv, i, None), out_dtype=F32, n_scaled=d)
            qkv = qkv.reshape(b, seq, 3 * d)
            att = _attention(
                qkv, qkv, jnp.asarray(slopes), no_sinks,
                q_block=lambda jb: jb, k_block=lambda jb: n_blocks + jb,
                v_block=lambda jb: 2 * n_blocks + jb,
                patterns=dil_patterns, use_sinks=False)
            w_o = (a_w_o, i, None)
        else:
            xs, q = _block(xs, w_in, w_out, ln_g[i, 0], ln_b[i, 0], seq=seq,
                           in_order=order, out_order="nat",
                           proj=((b_w_q, j, "head_cols"), Q_SCALE, BF16))
            order = "nat"
            att = _attention(
                q.reshape(b, seq, d), kv, jnp.asarray(slopes[head_order]),
                b_sinks[j][head_order],
                q_block=lambda jb: jb, k_block=lambda jb: jb // GROUP_B,
                v_block=lambda jb: kv_blocks + jb // GROUP_B,
                patterns=swa_patterns, use_sinks=True)
            w_o = (b_w_o, j, "head_rows")
        share_kv = i == N_A_LAYERS - 1
        out = _block(xs, (ffn2_w_in, i, "gate_up"), (ffn2_w_out, i, None),
                     ln_g[i, 2], ln_b[i, 2], seq=seq, in_order=order, out_order="nat",
                     attn=(att.reshape(t, d), w_o, ln_g[i, 1], ln_b[i, 1]),
                     proj=((kv_w, None, None), None, BF16) if share_kv else None)
        order = "nat"
        if share_kv:
            xs, kv = out
            kv = kv.reshape(b, seq, 2 * N_KV_B * HEAD_DIM)
        else:
            xs = out
    return xs.reshape(b, seq, d)
```

```python
import functools
import math

import numpy as np
import jax
import jax.numpy as jnp
from jax import lax
from jax.experimental import pallas as pl
from jax.experimental.pallas import tpu as pltpu

D_MODEL = 1024
DEPTH = 2
HEAD_DIM = 64
N_HEADS = D_MODEL // HEAD_DIM
N_KV_B = 4
GROUP_B = N_HEADS // N_KV_B
D_FF = 2816
DILATED_PATTERNS = ((128, 1), (512, 4), (2048, 16))
MAX_DILATION = max(d for _, d in DILATED_PATTERNS)
WINDOW_B = 128
BLOCK = 128
N_A_LAYERS = DEPTH // 2
ALPHA = (2.0 * DEPTH) ** 0.25
LN_EPS = 1e-5

LANES = 128
HEADS_PER_BLOCK = LANES // HEAD_DIM
MASK_VALUE = -1e30
LOG2E = math.log2(math.e)
Q_SCALE = HEAD_DIM ** -0.5 * LOG2E
VMEM_LIMIT = 56 * 1024 * 1024
ROW_TILE = 512
PROJ_ROWS = 1024

BF16 = jnp.bfloat16
F32 = jnp.float32


def _layer_norm(z, g, b):
    mu = jnp.mean(z, axis=-1, keepdims=True)
    zc = z - mu
    var = jnp.mean(zc * zc, axis=-1, keepdims=True)
    return zc * lax.rsqrt(var + LN_EPS) * g + b


def _resident(shape):
    return pl.BlockSpec(shape, lambda *_: (0,) * len(shape), pipeline_mode=pl.Buffered(1))


CLASS_ROWS = ROW_TILE // MAX_DILATION
N_SLABS = D_MODEL // LANES
W_STEPS = 16


def _tile_spec(order, seq, w, n_tiles, lag=0, rows=ROW_TILE):
    tile = lambda i: jnp.clip(i - W_STEPS - lag, 0, n_tiles - 1)
    if order == "nat":
        return pl.BlockSpec((rows, w), lambda i: (tile(i), 0))
    assert rows == ROW_TILE
    tiles_per_batch = seq // ROW_TILE

    def index(i):
        i = tile(i)
        return (i // tiles_per_batch, 0, i % tiles_per_batch, 0)

    return pl.BlockSpec((None, MAX_DILATION, CLASS_ROWS, w), index)


def _tile_view(x, order, seq):
    t, w = x.shape
    if order == "nat":
        return x
    return x.reshape(t // seq, MAX_DILATION, seq // MAX_DILATION, w)


def _swa_head(p):
    per = HEADS_PER_BLOCK * GROUP_B
    return (HEADS_PER_BLOCK * (p // per) + p % HEADS_PER_BLOCK) * GROUP_B + (p % per) // HEADS_PER_BLOCK


def _weight_chunk_spec(w, layer, layout):
    rows, cols = w.shape[-2] // W_STEPS, w.shape[-1]
    chunk = lambda i: jnp.minimum(i, W_STEPS - 1)
    if layout == "head_rows":
        assert rows == HEAD_DIM and W_STEPS == N_HEADS
        chunk = lambda i: _swa_head(jnp.minimum(i, W_STEPS - 1))
    if layer is None:
        return pl.BlockSpec((rows, cols), lambda i: (chunk(i), 0))
    return pl.BlockSpec((None, rows, cols), lambda i: (layer, chunk(i), 0))


def _weight_scratch(w):
    return pltpu.VMEM(w.shape[-2:], BF16)


def _load_weight_chunks(step, entries):
    for src_ref, dst_ref, layout in entries:
        rows, cols = src_ref.shape
        dst_rows = pl.ds(pl.multiple_of(step * rows, rows), rows)
        if layout == "gate_up":
            half = cols // 2
            for k in range(half // LANES):
                lanes = slice(k * LANES, (k + 1) * LANES)
                dst_ref[dst_rows, 2 * k * LANES:(2 * k + 1) * LANES] = (
                    src_ref[:, lanes].astype(BF16))
                dst_ref[dst_rows, (2 * k + 1) * LANES:(2 * k + 2) * LANES] = (
                    src_ref[:, half + k * LANES:half + (k + 1) * LANES].astype(BF16))
        elif layout == "head_cols":
            for blk in range(cols // LANES):
                heads = [_swa_head(HEADS_PER_BLOCK * blk + e) for e in range(HEADS_PER_BLOCK)]
                dst_ref[dst_rows, blk * LANES:(blk + 1) * LANES] = jnp.concatenate(
                    [src_ref[:, h * HEAD_DIM:(h + 1) * HEAD_DIM] for h in heads],
                    axis=1).astype(BF16)
        else:
            dst_ref[dst_rows, :] = src_ref[...].astype(BF16)


SUB = 4
SUB_ROWS = ROW_TILE // SUB
assert SUB * SUB == MAX_DILATION


def _block_kernel(*refs, has_attn, has_proj, proj_scale, proj_layout, in_order, out_order):
    refs = list(refs)
    x_ref = refs.pop(0)
    if has_attn:
        a_ref, wo_src, g1_ref, b1_ref = refs[:4]
        del refs[:4]
    win_src, wout_src, g2_ref, b2_ref = refs[:4]
    del refs[:4]
    wp_src = refs.pop(0) if has_proj else None
    o_ref = refs.pop(0)
    p_ref = refs.pop(0) if has_proj else None
    win_ref, wout_ref = refs[:2]
    del refs[:2]
    weights = [(win_src, win_ref, "gate_up"), (wout_src, wout_ref, None)]
    if has_attn:
        wo_ref = refs.pop(0)
        weights.append((wo_src, wo_ref, None))
    if has_proj:
        wp_ref = refs.pop(0)
        weights.append((wp_src, wp_ref, proj_layout))
    slab_ref = refs.pop(0) if in_order != out_order else None
    z_ref = refs.pop(0)
    step = pl.program_id(0)
    last = pl.num_programs(0) - 1

    @pl.when(step < W_STEPS)
    def _():
        _load_weight_chunks(step, weights)

    @pl.when(step == 0)
    def _():
        z_ref[...] = jnp.zeros_like(z_ref)

    def compute():
        d = x_ref.shape[-1]
        x = x_ref[...].reshape(ROW_TILE, d)
        if has_attn:
            a = a_ref[...].reshape(ROW_TILE, a_ref.shape[-1])
            mix = jnp.dot(a.astype(BF16), wo_ref[...], preferred_element_type=F32)
            x = _layer_norm(ALPHA * x + mix, g1_ref[...], b1_ref[...])
        h = jnp.dot(x.astype(BF16), win_ref[...], preferred_element_type=F32)
        acts = []
        for k in range(D_FF // LANES):
            gate = h[:, 2 * k * LANES:(2 * k + 1) * LANES]
            up = h[:, (2 * k + 1) * LANES:(2 * k + 2) * LANES]
            acts.append((gate * jax.nn.sigmoid(gate) * up).astype(BF16))
        act = jnp.concatenate(acts, axis=1)
        ffn = jnp.dot(act, wout_ref[...], preferred_element_type=F32)
        z_ref[...] = ALPHA * x + 0.5 * ffn

    def finish():
        y = _layer_norm(z_ref[...], g2_ref[...], b2_ref[...])
        if in_order == out_order:
            o_ref[...] = y.reshape(o_ref.shape)
        elif out_order == "cm":
            for j in range(N_SLABS):
                slab_ref[0, j] = y[:, j * LANES:(j + 1) * LANES]
            for j in range(N_SLABS):
                for r1 in range(SUB):
                    slab_ref[1, j, r1 * SUB_ROWS:(r1 + 1) * SUB_ROWS, :] = (
                        slab_ref[0, j, pl.ds(r1, SUB_ROWS, stride=SUB), :])
            for r1 in range(SUB):
                for r2 in range(SUB):
                    o_ref[SUB * r2 + r1] = jnp.concatenate(
                        [slab_ref[1, j, pl.ds(r1 * SUB_ROWS + r2, CLASS_ROWS, stride=SUB), :]
                         for j in range(N_SLABS)], axis=1)
        else:
            for r1 in range(SUB):
                for r2 in range(SUB):
                    r = SUB * r2 + r1
                    for j in range(N_SLABS):
                        slab_ref[1, j, pl.ds(r1 * SUB_ROWS + r2, CLASS_ROWS, stride=SUB), :] = (
                            y[r * CLASS_ROWS:(r + 1) * CLASS_ROWS, j * LANES:(j + 1) * LANES])
            for j in range(N_SLABS):
                for r1 in range(SUB):
                    slab_ref[0, j, pl.ds(r1, SUB_ROWS, stride=SUB), :] = (
                        slab_ref[1, j, r1 * SUB_ROWS:(r1 + 1) * SUB_ROWS, :])
            y = jnp.concatenate([slab_ref[0, j] for j in range(N_SLABS)], axis=1)
            o_ref[...] = y
        if has_proj:
            assert out_order == "nat"
            res = jnp.dot(y.astype(BF16), wp_ref[...], preferred_element_type=F32)
            if proj_scale is not None:
                res = res * proj_scale
            p_ref[...] = res.astype(p_ref.dtype)

    @pl.when((step >= W_STEPS) & (step < last))
    def _():
        finish()
        compute()

    @pl.when(step == last)
    def _():
        finish()


def _block(x, w_in, w_out, g, b, *, seq, in_order, out_order, attn=None, proj=None):
    t, d = x.shape
    n_tiles = t // ROW_TILE
    vec = lambda v: v.reshape(1, d)
    in_spec = lambda w: _tile_spec(in_order, seq, w, n_tiles)
    out_spec = lambda order, w: _tile_spec(order, seq, w, n_tiles, lag=1)
    args, specs = [_tile_view(x, in_order, seq)], [in_spec(d)]
    scratch = [_weight_scratch(w_in[0]), _weight_scratch(w_out[0])]
    if attn is not None:
        att, w_o, g1, b1 = attn
        args += [_tile_view(att, in_order, seq), w_o[0], vec(g1), vec(b1)]
        specs += [in_spec(d), _weight_chunk_spec(*w_o), _resident((1, d)), _resident((1, d))]
        scratch.append(_weight_scratch(w_o[0]))
    args += [w_in[0], w_out[0], vec(g), vec(b)]
    specs += [_weight_chunk_spec(*w_in), _weight_chunk_spec(*w_out),
              _resident((1, d)), _resident((1, d))]
    out_shapes = [jax.ShapeDtypeStruct(_tile_view(x, out_order, seq).shape, F32)]
    out_specs = [out_spec(out_order, d)]
    proj_scale = proj_layout = None
    if proj is not None:
        w_p, proj_scale, p_dtype = proj
        proj_layout = w_p[2]
        n_p = w_p[0].shape[-1]
        args.append(w_p[0])
        specs.append(_weight_chunk_spec(*w_p))
        scratch.append(_weight_scratch(w_p[0]))
        out_shapes.append(jax.ShapeDtypeStruct((t, n_p), p_dtype))
        out_specs.append(out_spec("nat", n_p))
    if in_order != out_order:
        scratch.append(pltpu.VMEM((2, N_SLABS, ROW_TILE, LANES), F32))
    scratch.append(pltpu.VMEM((ROW_TILE, d), F32))
    outs = pl.pallas_call(
        functools.partial(_block_kernel, has_attn=attn is not None, has_proj=proj is not None,
                          proj_scale=proj_scale, proj_layout=proj_layout,
                          in_order=in_order, out_order=out_order),
        grid=(W_STEPS + n_tiles + 1,),
        in_specs=specs,
        out_specs=out_specs,
        out_shape=out_shapes,
        scratch_shapes=scratch,
        compiler_params=pltpu.CompilerParams(
            dimension_semantics=("arbitrary",), vmem_limit_bytes=VMEM_LIMIT),
        name="block_attn" if attn is not None else "block_ffn",
    )(*args)
    y = outs[0].reshape(t, d)
    return (y, outs[1]) if proj is not None else y


def _proj_kernel(x_ref, w_src, o_ref, w_ref, *, n_scaled):
    step = pl.program_id(0)

    @pl.when(step < W_STEPS)
    def _():
        _load_weight_chunks(step, [(w_src, w_ref, None)])

    @pl.when(step >= W_STEPS)
    def _():
        res = jnp.dot(x_ref[...].astype(BF16), w_ref[...], preferred_element_type=F32)
        if n_scaled:
            col = lax.broadcasted_iota(jnp.int32, (1, res.shape[1]), 1)
            res = res * jnp.where(col < n_scaled, Q_SCALE, 1.0)
        o_ref[...] = res.astype(o_ref.dtype)


def _proj(x, w, *, out_dtype, n_scaled=0):
    t, d = x.shape
    n = w[0].shape[-1]
    n_tiles = t // PROJ_ROWS
    return pl.pallas_call(
        functools.partial(_proj_kernel, n_scaled=n_scaled),
        grid=(W_STEPS + n_tiles,),
        in_specs=[_tile_spec("nat", 0, d, n_tiles, rows=PROJ_ROWS), _weight_chunk_spec(*w)],
        out_specs=_tile_spec("nat", 0, n, n_tiles, rows=PROJ_ROWS),
        out_shape=jax.ShapeDtypeStruct((t, n), out_dtype),
        scratch_shapes=[_weight_scratch(w[0])],
        compiler_params=pltpu.CompilerParams(
            dimension_semantics=("arbitrary",), vmem_limit_bytes=VMEM_LIMIT),
        name="proj",
    )(x, w[0])


def _attn_kernel(slope_ref, sink_ref, q_ref, k_ref, v_ref, o_ref, bias_ref, *scratch,
                 patterns, use_sinks):
    seq = q_ref.shape[0]
    blk = pl.program_id(0)
    multi = len(patterns) > 1
    if multi:
        acc_ref, max_ref, den_ref = scratch

    lane = lax.broadcasted_iota(jnp.int32, (BLOCK, LANES), 1)
    low_half = lane < HEAD_DIM
    head_sel = [jnp.where((lane >= h * HEAD_DIM) & (lane < (h + 1) * HEAD_DIM), 1.0, 0.0).astype(BF16)
                for h in range(HEADS_PER_BLOCK)]
    sel2 = [jnp.concatenate([sel, sel], axis=0) for sel in head_sel]

    @pl.when(pl.program_id(1) == 0)
    def _():
        row = lax.broadcasted_iota(jnp.int32, (BLOCK, 2 * BLOCK), 0)
        col = lax.broadcasted_iota(jnp.int32, (BLOCK, 2 * BLOCK), 1)
        in_cur = col >= BLOCK
        for pi, (n_chunks, _, max_dist, scale) in enumerate(patterns):
            chunk_len = BLOCK // n_chunks
            sh = chunk_len.bit_length() - 1
            pos = lambda x: n_chunks * (x & (chunk_len - 1)) + (x >> sh)
            dist = pos(row) - pos(col & (BLOCK - 1)) + jnp.where(in_cur, 0, BLOCK)
            valid = (dist >= 0) & (dist <= max_dist)
            for h in range(HEADS_PER_BLOCK):
                slope = slope_ref[blk * HEADS_PER_BLOCK + h]
                bias = jnp.where(valid, (-slope * scale * LOG2E) * dist.astype(F32), MASK_VALUE)
                bias_ref[(pi * 2 + h) * 2 + 0] = bias
                bias_ref[(pi * 2 + h) * 2 + 1] = jnp.where(in_cur, bias, MASK_VALUE)

    if use_sinks:
        sink2 = jnp.where(low_half, sink_ref[blk * HEADS_PER_BLOCK] * LOG2E,
                          sink_ref[blk * HEADS_PER_BLOCK + 1] * LOG2E)

    def block_task(pi, cls, n):
        n_chunks, n_classes, _, _ = patterns[pi]
        chunk_len = BLOCK // n_chunks
        class_rows = seq // (n_chunks * n_classes)
        static = isinstance(n, int)

        def chunks(n):
            starts = [(n_classes * c + cls) * class_rows + n * chunk_len for c in range(n_chunks)]
            return [pl.ds(s if static else pl.multiple_of(s, chunk_len), chunk_len)
                    for s in starts]

        def load(ref, slices):
            parts = [ref[s, :] for s in slices]
            return parts[0] if len(parts) == 1 else jnp.concatenate(parts, axis=0)

        def store(ref, slices, val):
            for c, s in enumerate(slices):
                part = val[c * chunk_len:(c + 1) * chunk_len]
                if multi:
                    ref[pi, s, :] = part
                else:
                    ref[s, :] = part.astype(ref.dtype)

        if static:
            first, prev_n = int(n == 0), max(n - 1, 0)
        else:
            first, prev_n = (n == 0).astype(jnp.int32), jnp.maximum(n - 1, 0)
        cur = chunks(n)
        prev = chunks(prev_n)
        q = load(q_ref, cur).astype(BF16)
        kk = jnp.concatenate([load(k_ref, prev), load(k_ref, cur)], axis=0).astype(BF16)
        vv = jnp.concatenate([load(v_ref, prev), load(v_ref, cur)], axis=0).astype(BF16)
        q2 = jnp.concatenate([q * head_sel[h] for h in range(HEADS_PER_BLOCK)], axis=0)
        s2 = lax.dot_general(q2, kk, (((1,), (1,)), ((), ())),
                             preferred_element_type=F32)
        ps, ms = [], []
        for h in range(HEADS_PER_BLOCK):
            s = s2[h * BLOCK:(h + 1) * BLOCK] + bias_ref[(pi * 2 + h) * 2 + first]
            m = jnp.max(s, axis=-1, keepdims=True)
            ps.append(jnp.exp2(s - m).astype(BF16))
            ms.append(m)
        rhs = jnp.concatenate(
            [jnp.concatenate([vv * sel2[h], sel2[h]], axis=1) for h in range(HEADS_PER_BLOCK)],
            axis=0)
        nd = jnp.dot(jnp.concatenate(ps, axis=1), rhs, preferred_element_type=F32)
        num, den = nd[:, :LANES], nd[:, LANES:]
        m2 = jnp.where(low_half, ms[0], ms[1])
        if multi:
            store(acc_ref, cur, num)
            store(max_ref, cur, m2)
            store(den_ref, cur, den)
        else:
            if use_sinks:
                den = den + jnp.exp2(sink2 - m2)
            store(o_ref, cur, num / den)

    def combine(row0, n_rows):
        rows = pl.ds(row0, n_rows)
        mxs = [max_ref[pi, rows, :] for pi in range(len(patterns))]
        mx = functools.reduce(jnp.maximum, mxs)
        es = [jnp.exp2(m - mx) for m in mxs]
        add = lambda a, b: a + b
        num = functools.reduce(add, [e * acc_ref[pi, rows, :] for pi, e in enumerate(es)])
        tot = functools.reduce(add, [e * den_ref[pi, rows, :] for pi, e in enumerate(es)])
        o_ref[rows, :] = (num / tot).astype(o_ref.dtype)

    for pi, (n_chunks, n_classes, _, _) in enumerate(patterns):
        class_rows = seq // (n_chunks * n_classes)
        blocks_per_class = class_rows // (BLOCK // n_chunks)
        shift = blocks_per_class.bit_length() - 1
        assert blocks_per_class == 1 << shift and n_classes * blocks_per_class == seq // BLOCK
        if multi and pi == len(patterns) - 1:
            assert n_chunks == 1
            for cls in range(n_classes):
                for n in range(blocks_per_class):
                    block_task(pi, cls, n)
                combine(cls * class_rows, class_rows)
        else:
            def task(t, carry, pi=pi, shift=shift, blocks_per_class=blocks_per_class):
                block_task(pi, lax.shift_right_logical(t, shift),
                           lax.bitwise_and(t, blocks_per_class - 1))
                return carry

            lax.fori_loop(0, seq // BLOCK, task, 0, unroll=True)


def _attention(q_arr, kv_arr, slopes, sinks, *, q_block, k_block, v_block, patterns, use_sinks):
    b, seq, _ = q_arr.shape
    n_blocks = N_HEADS // HEADS_PER_BLOCK
    n_pat = len(patterns)
    scratch = [pltpu.VMEM((n_pat * 4, BLOCK, 2 * BLOCK), F32)]
    if n_pat > 1:
        scratch += [pltpu.VMEM((n_pat, seq, LANES), F32) for _ in range(3)]
    smem = pl.BlockSpec(memory_space=pltpu.SMEM)
    return pl.pallas_call(
        functools.partial(_attn_kernel, patterns=patterns, use_sinks=use_sinks),
        grid=(n_blocks, b),
        in_specs=[
            smem, smem,
            pl.BlockSpec((None, seq, LANES), lambda j, i: (i, 0, q_block(j))),
            pl.BlockSpec((None, seq, LANES), lambda j, i: (i, 0, k_block(j))),
            pl.BlockSpec((None, seq, LANES), lambda j, i: (i, 0, v_block(j))),
        ],
        out_specs=pl.BlockSpec((None, seq, LANES), lambda j, i: (i, 0, j)),
        out_shape=jax.ShapeDtypeStruct((b, seq, D_MODEL), BF16),
        scratch_shapes=scratch,
        compiler_params=pltpu.CompilerParams(
            dimension_semantics=("arbitrary", "arbitrary"), vmem_limit_bytes=VMEM_LIMIT),
        name="attn_dilated" if n_pat > 1 else "attn_swa",
    )(slopes, sinks, q_arr, kv_arr, kv_arr)


def _alibi_slopes(n):
    return np.array([2.0 ** (-8.0 * (h + 1) / n) for h in range(n)], dtype=np.float32)


def kernel(x, ffn1_w_in, ffn1_w_out, ffn2_w_in, ffn2_w_out, ln_g, ln_b,
           a_w_qkv, a_w_o, kv_w, b_w_q, b_sinks, b_w_o):
    b, seq, d = x.shape
    t = b * seq
    assert x.dtype == F32 and d == D_MODEL and ffn1_w_in.shape[-1] == 2 * D_FF
    assert seq % (MAX_DILATION * BLOCK) == 0 and seq % ROW_TILE == 0 and t % PROJ_ROWS == 0
    assert all(n % W_STEPS == 0 for n in (d, D_FF)) and (D_FF // W_STEPS) % 16 == 0
    xs = x.reshape(t, d)
    slopes = _alibi_slopes(N_HEADS)
    no_sinks = jnp.zeros((N_HEADS,), F32)
    dil_patterns = tuple((MAX_DILATION // dil, dil, win // dil, float(dil))
                         for win, dil in DILATED_PATTERNS)
    swa_patterns = ((1, 1, WINDOW_B - 1, 1.0),)
    n_blocks = N_HEADS // HEADS_PER_BLOCK
    head_order = np.array([_swa_head(p) for p in range(N_HEADS)], dtype=np.int32)
    kv_blocks = N_KV_B // HEADS_PER_BLOCK
    kv = None
    order = "nat"
    for i in range(DEPTH):
        dilated = i < N_A_LAYERS
        j = i - N_A_LAYERS
        w_in, w_out = (ffn1_w_in, i, "gate_up"), (ffn1_w_out, i, None)
        if dilated:
            xs = _block(xs, w_in, w_out, ln_g[i, 0], ln_b[i, 0], seq=seq,
                        in_order=order, out_order="cm")
            order = "cm"
            qkv = _proj(xs, (a_w_qkv, i, None), out_dtype=F32, n_scaled=d)
            qkv = qkv.reshape(b, seq, 3 * d)
            att = _attention(
                qkv, qkv, jnp.asarray(slopes), no_sinks,
                q_block=lambda jb: jb, k_block=lambda jb: n_blocks + jb,
                v_block=lambda jb: 2 * n_blocks + jb,
                patterns=dil_patterns, use_sinks=False)
            w_o = (a_w_o, i, None)
        else:
            xs, q = _block(xs, w_in, w_out, ln_g[i, 0], ln_b[i, 0], seq=seq,
                           in_order=order, out_order="nat",
                           proj=((b_w_q, j, "head_cols"), Q_SCALE, BF16))
            order = "nat"
            att = _attention(
                q.reshape(b, seq, d), kv, jnp.asarray(slopes[head_order]),
                b_sinks[j][head_order],
                q_block=lambda jb: jb, k_block=lambda jb: jb // GROUP_B,
                v_block=lambda jb: kv_blocks + jb // GROUP_B,
                patterns=swa_patterns, use_sinks=True)
            w_o = (b_w_o, j, "head_rows")
        share_kv = i == N_A_LAYERS - 1
        out = _block(xs, (ffn2_w_in, i, "gate_up"), (ffn2_w_out, i, None),
                     ln_g[i, 2], ln_b[i, 2], seq=seq, in_order=order, out_order="nat",
                     attn=(att.reshape(t, d), w_o, ln_g[i, 1], ln_b[i, 1]),
                     proj=((kv_w, None, None), None, BF16) if share_kv else None)
        order = "nat"
        if share_kv:
            xs, kv = out
            kv = kv.reshape(b, seq, 2 * N_KV_B * HEAD_DIM)
        else:
            xs = out
    return xs.reshape(b, seq, d)
```

```python
import functools
import math

import numpy as np
import jax
import jax.numpy as jnp
from jax import lax
from jax.experimental import pallas as pl
from jax.experimental.pallas import tpu as pltpu

D_MODEL = 1024
DEPTH = 2
HEAD_DIM = 64
N_HEADS = D_MODEL // HEAD_DIM
N_KV_B = 4
GROUP_B = N_HEADS // N_KV_B
D_FF = 2816
DILATED_PATTERNS = ((128, 1), (512, 4), (2048, 16))
MAX_DILATION = max(d for _, d in DILATED_PATTERNS)
WINDOW_B = 128
BLOCK = 128
N_A_LAYERS = DEPTH // 2
ALPHA = (2.0 * DEPTH) ** 0.25
LN_EPS = 1e-5

LANES = 128
HEADS_PER_BLOCK = LANES // HEAD_DIM
MASK_VALUE = -1e30
LOG2E = math.log2(math.e)
Q_SCALE = HEAD_DIM ** -0.5 * LOG2E
VMEM_LIMIT = 56 * 1024 * 1024
ROW_TILE = 512
PROJ_ROWS = 1024

BF16 = jnp.bfloat16
F32 = jnp.float32


def _layer_norm(z, g, b):
    mu = jnp.mean(z, axis=-1, keepdims=True)
    zc = z - mu
    var = jnp.mean(zc * zc, axis=-1, keepdims=True)
    return zc * lax.rsqrt(var + LN_EPS) * g + b


def _resident(shape):
    return pl.BlockSpec(shape, lambda *_: (0,) * len(shape), pipeline_mode=pl.Buffered(1))


CLASS_ROWS = ROW_TILE // MAX_DILATION
N_SLABS = D_MODEL // LANES
W_STEPS = 16


def _tile_spec(order, seq, w, n_tiles, lag=0, rows=ROW_TILE):
    tile = lambda i: jnp.clip(i - W_STEPS - lag, 0, n_tiles - 1)
    if order == "nat":
        return pl.BlockSpec((rows, w), lambda i: (tile(i), 0))
    assert rows == ROW_TILE
    tiles_per_batch = seq // ROW_TILE

    def index(i):
        i = tile(i)
        return (i // tiles_per_batch, 0, i % tiles_per_batch, 0)

    return pl.BlockSpec((None, MAX_DILATION, CLASS_ROWS, w), index)


def _tile_view(x, order, seq):
    t, w = x.shape
    if order == "nat":
        return x
    return x.reshape(t // seq, MAX_DILATION, seq // MAX_DILATION, w)


def _swa_head(p):
    per = HEADS_PER_BLOCK * GROUP_B
    return (HEADS_PER_BLOCK * (p // per) + p % HEADS_PER_BLOCK) * GROUP_B + (p % per) // HEADS_PER_BLOCK


def _weight_chunk_spec(w, layer, layout):
    rows, cols = w.shape[-2] // W_STEPS, w.shape[-1]
    chunk = lambda i: jnp.minimum(i, W_STEPS - 1)
    if layout == "head_rows":
        assert rows == HEAD_DIM and W_STEPS == N_HEADS
        chunk = lambda i: _swa_head(jnp.minimum(i, W_STEPS - 1))
    if layer is None:
        return pl.BlockSpec((rows, cols), lambda i: (chunk(i), 0))
    return pl.BlockSpec((None, rows, cols), lambda i: (layer, chunk(i), 0))


def _weight_scratch(w):
    return pltpu.VMEM(w.shape[-2:], BF16)


def _load_weight_chunks(step, entries):
    for src_ref, dst_ref, layout in entries:
        rows, cols = src_ref.shape
        dst_rows = pl.ds(pl.multiple_of(step * rows, rows), rows)
        if layout == "gate_up":
            half = cols // 2
            for k in range(half // LANES):
                lanes = slice(k * LANES, (k + 1) * LANES)
                dst_ref[dst_rows, 2 * k * LANES:(2 * k + 1) * LANES] = (
                    src_ref[:, lanes].astype(BF16))
                dst_ref[dst_rows, (2 * k + 1) * LANES:(2 * k + 2) * LANES] = (
                    src_ref[:, half + k * LANES:half + (k + 1) * LANES].astype(BF16))
        elif layout == "head_cols":
            for blk in range(cols // LANES):
                heads = [_swa_head(HEADS_PER_BLOCK * blk + e) for e in range(HEADS_PER_BLOCK)]
                dst_ref[dst_rows, blk * LANES:(blk + 1) * LANES] = jnp.concatenate(
                    [src_ref[:, h * HEAD_DIM:(h + 1) * HEAD_DIM] for h in heads],
                    axis=1).astype(BF16)
        else:
            dst_ref[dst_rows, :] = src_ref[...].astype(BF16)


SUB = 4
SUB_ROWS = ROW_TILE // SUB
assert SUB * SUB == MAX_DILATION


def _block_kernel(*refs, has_attn, has_proj, proj_scale, proj_layout, in_order, out_order):
    refs = list(refs)
    x_ref = refs.pop(0)
    if has_attn:
        a_ref, wo_src, g1_ref, b1_ref = refs[:4]
        del refs[:4]
    win_src, wout_src, g2_ref, b2_ref = refs[:4]
    del refs[:4]
    wp_src = refs.pop(0) if has_proj else None
    o_ref = refs.pop(0)
    p_ref = refs.pop(0) if has_proj else None
    win_ref, wout_ref = refs[:2]
    del refs[:2]
    weights = [(win_src, win_ref, "gate_up"), (wout_src, wout_ref, None)]
    if has_attn:
        wo_ref = refs.pop(0)
        weights.append((wo_src, wo_ref, None))
    if has_proj:
        wp_ref = refs.pop(0)
        weights.append((wp_src, wp_ref, proj_layout))
    slab_ref = refs.pop(0) if in_order != out_order else None
    z_ref = refs.pop(0)
    step = pl.program_id(0)
    last = pl.num_programs(0) - 1

    @pl.when(step < W_STEPS)
    def _():
        _load_weight_chunks(step, weights)

    @pl.when(step == 0)
    def _():
        z_ref[...] = jnp.zeros_like(z_ref)

    def compute():
        d = x_ref.shape[-1]
        x = x_ref[...].reshape(ROW_TILE, d)
        if has_attn:
            a = a_ref[...].reshape(ROW_TILE, a_ref.shape[-1])
            mix = jnp.dot(a.astype(BF16), wo_ref[...], preferred_element_type=F32)
            x = _layer_norm(ALPHA * x + mix, g1_ref[...], b1_ref[...])
        h = jnp.dot(x.astype(BF16), win_ref[...], preferred_element_type=F32)
        acts = []
        for k in range(D_FF // LANES):
            gate = h[:, 2 * k * LANES:(2 * k + 1) * LANES]
            up = h[:, (2 * k + 1) * LANES:(2 * k + 2) * LANES]
            acts.append((gate * jax.nn.sigmoid(gate) * up).astype(BF16))
        act = jnp.concatenate(acts, axis=1)
        ffn = jnp.dot(act, wout_ref[...], preferred_element_type=F32)
        z_ref[...] = ALPHA * x + 0.5 * ffn

    def finish():
        y = _layer_norm(z_ref[...], g2_ref[...], b2_ref[...])
        if in_order == out_order:
            o_ref[...] = y.reshape(o_ref.shape)
        elif out_order == "cm":
            for j in range(N_SLABS):
                slab_ref[0, j] = y[:, j * LANES:(j + 1) * LANES]
            for j in range(N_SLABS):
                for r1 in range(SUB):
                    slab_ref[1, j, r1 * SUB_ROWS:(r1 + 1) * SUB_ROWS, :] = (
                        slab_ref[0, j, pl.ds(r1, SUB_ROWS, stride=SUB), :])
            for r1 in range(SUB):
                for r2 in range(SUB):
                    o_ref[SUB * r2 + r1] = jnp.concatenate(
                        [slab_ref[1, j, pl.ds(r1 * SUB_ROWS + r2, CLASS_ROWS, stride=SUB), :]
                         for j in range(N_SLABS)], axis=1)
        else:
            for r1 in range(SUB):
                for r2 in range(SUB):
                    r = SUB * r2 + r1
                    for j in range(N_SLABS):
                        slab_ref[1, j, pl.ds(r1 * SUB_ROWS + r2, CLASS_ROWS, stride=SUB), :] = (
                            y[r * CLASS_ROWS:(r + 1) * CLASS_ROWS, j * LANES:(j + 1) * LANES])
            for j in range(N_SLABS):
                for r1 in range(SUB):
                    slab_ref[0, j, pl.ds(r1, SUB_ROWS, stride=SUB), :] = (
                        slab_ref[1, j, r1 * SUB_ROWS:(r1 + 1) * SUB_ROWS, :])
            y = jnp.concatenate([slab_ref[0, j] for j in range(N_SLABS)], axis=1)
            o_ref[...] = y
        if has_proj:
            assert out_order == "nat"
            res = jnp.dot(y.astype(BF16), wp_ref[...], preferred_element_type=F32)
            if proj_scale is not None:
                res = res * proj_scale
            p_ref[...] = res.astype(p_ref.dtype)

    @pl.when((step >= W_STEPS) & (step < last))
    def _():
        finish()
        compute()

    @pl.when(step == last)
    def _():
        finish()


def _block(x, w_in, w_out, g, b, *, seq, in_order, out_order, attn=None, proj=None):
    t, d = x.shape
    n_tiles = t // ROW_TILE
    vec = lambda v: v.reshape(1, d)
    in_spec = lambda w: _tile_spec(in_order, seq, w, n_tiles)
    out_spec = lambda order, w: _tile_spec(order, seq, w, n_tiles, lag=1)
    args, specs = [_tile_view(x, in_order, seq)], [in_spec(d)]
    scratch = [_weight_scratch(w_in[0]), _weight_scratch(w_out[0])]
    if attn is not None:
        att, w_o, g1, b1 = attn
        args += [_tile_view(att, in_order, seq), w_o[0], vec(g1), vec(b1)]
        specs += [in_spec(d), _weight_chunk_spec(*w_o), _resident((1, d)), _resident((1, d))]
        scratch.append(_weight_scratch(w_o[0]))
    args += [w_in[0], w_out[0], vec(g), vec(b)]
    specs += [_weight_chunk_spec(*w_in), _weight_chunk_spec(*w_out),
              _resident((1, d)), _resident((1, d))]
    out_shapes = [jax.ShapeDtypeStruct(_tile_view(x, out_order, seq).shape, F32)]
    out_specs = [out_spec(out_order, d)]
    proj_scale = proj_layout = None
    if proj is not None:
        w_p, proj_scale, p_dtype = proj
        proj_layout = w_p[2]
        n_p = w_p[0].shape[-1]
        args.append(w_p[0])
        specs.append(_weight_chunk_spec(*w_p))
        scratch.append(_weight_scratch(w_p[0]))
        out_shapes.append(jax.ShapeDtypeStruct((t, n_p), p_dtype))
        out_specs.append(out_spec("nat", n_p))
    if in_order != out_order:
        scratch.append(pltpu.VMEM((2, N_SLABS, ROW_TILE, LANES), F32))
    scratch.append(pltpu.VMEM((ROW_TILE, d), F32))
    outs = pl.pallas_call(
        functools.partial(_block_kernel, has_attn=attn is not None, has_proj=proj is not None,
                          proj_scale=proj_scale, proj_layout=proj_layout,
                          in_order=in_order, out_order=out_order),
        grid=(W_STEPS + n_tiles + 1,),
        in_specs=specs,
        out_specs=out_specs,
        out_shape=out_shapes,
        scratch_shapes=scratch,
        compiler_params=pltpu.CompilerParams(
            dimension_semantics=("arbitrary",), vmem_limit_bytes=VMEM_LIMIT),
        name="block_attn" if attn is not None else "block_ffn",
    )(*args)
    y = outs[0].reshape(t, d)
    return (y, outs[1]) if proj is not None else y


def _proj_kernel(x_ref, w_src, o_ref, w_ref, *, n_scaled):
    step = pl.program_id(0)

    @pl.when(step < W_STEPS)
    def _():
        _load_weight_chunks(step, [(w_src, w_ref, None)])

    @pl.when(step >= W_STEPS)
    def _():
        res = jnp.dot(x_ref[...].astype(BF16), w_ref[...], preferred_element_type=F32)
        if n_scaled:
            col = lax.broadcasted_iota(jnp.int32, (1, res.shape[1]), 1)
            res = res * jnp.where(col < n_scaled, Q_SCALE, 1.0)
        o_ref[...] = res.astype(o_ref.dtype)


def _proj(x, w, *, out_dtype, n_scaled=0):
    t, d = x.shape
    n = w[0].shape[-1]
    n_tiles = t // PROJ_ROWS
    return pl.pallas_call(
        functools.partial(_proj_kernel, n_scaled=n_scaled),
        grid=(W_STEPS + n_tiles,),
        in_specs=[_tile_spec("nat", 0, d, n_tiles, rows=PROJ_ROWS), _weight_chunk_spec(*w)],
        out_specs=_tile_spec("nat", 0, n, n_tiles, rows=PROJ_ROWS),
        out_shape=jax.ShapeDtypeStruct((t, n), out_dtype),
        scratch_shapes=[_weight_scratch(w[0])],
        compiler_params=pltpu.CompilerParams(
            dimension_semantics=("arbitrary",), vmem_limit_bytes=VMEM_LIMIT),
        name="proj",
    )(x, w[0])


def _attn_kernel(slope_ref, sink_ref, q_ref, k_ref, v_ref, o_ref, bias_ref, *scratch,
                 patterns, use_sinks):
    seq = q_ref.shape[0]
    blk = pl.program_id(0)
    multi = len(patterns) > 1
    if multi:
        acc_ref, max_ref, den_ref = scratch

    lane = lax.broadcasted_iota(jnp.int32, (BLOCK, LANES), 1)
    low_half = lane < HEAD_DIM
    head_sel = [jnp.where((lane >= h * HEAD_DIM) & (lane < (h + 1) * HEAD_DIM), 1.0, 0.0).astype(BF16)
                for h in range(HEADS_PER_BLOCK)]
    sel2 = [jnp.concatenate([sel, sel], axis=0) for sel in head_sel]

    fold = (not multi) and patterns[0][0] == 1 and patterns[0][2] < BLOCK
    if fold:
        own = (lax.broadcasted_iota(jnp.int32, (BLOCK, BLOCK), 1)
               <= lax.broadcasted_iota(jnp.int32, (BLOCK, BLOCK), 0))
        own_sel = jnp.where(own, 1.0, 0.0).astype(BF16)
        prev_sel = jnp.where(own, 0.0, 1.0).astype(BF16)

        def fold_keys(s):
            return jnp.where(own, s[:, BLOCK:], s[:, :BLOCK])

    @pl.when(pl.program_id(1) == 0)
    def _():
        row = lax.broadcasted_iota(jnp.int32, (BLOCK, 2 * BLOCK), 0)
        col = lax.broadcasted_iota(jnp.int32, (BLOCK, 2 * BLOCK), 1)
        in_cur = col >= BLOCK
        for pi, (n_chunks, _, max_dist, scale) in enumerate(patterns):
            chunk_len = BLOCK // n_chunks
            sh = chunk_len.bit_length() - 1
            pos = lambda x: n_chunks * (x & (chunk_len - 1)) + (x >> sh)
            dist = pos(row) - pos(col & (BLOCK - 1)) + jnp.where(in_cur, 0, BLOCK)
            valid = (dist >= 0) & (dist <= max_dist)
            for h in range(HEADS_PER_BLOCK):
                slope = slope_ref[blk * HEADS_PER_BLOCK + h]
                bias = jnp.where(valid, (-slope * scale * LOG2E) * dist.astype(F32), MASK_VALUE)
                first_bias = jnp.where(in_cur, bias, MASK_VALUE)
                if fold:
                    bias, first_bias = fold_keys(bias), fold_keys(first_bias)
                    bias_ref[(pi * 2 + h) * 2 + 0, :, :BLOCK] = bias
                    bias_ref[(pi * 2 + h) * 2 + 1, :, :BLOCK] = first_bias
                else:
                    bias_ref[(pi * 2 + h) * 2 + 0] = bias
                    bias_ref[(pi * 2 + h) * 2 + 1] = first_bias

    if use_sinks:
        sink2 = jnp.where(low_half, sink_ref[blk * HEADS_PER_BLOCK] * LOG2E,
                          sink_ref[blk * HEADS_PER_BLOCK + 1] * LOG2E)

    def block_task(pi, cls, n):
        n_chunks, n_classes, _, _ = patterns[pi]
        chunk_len = BLOCK // n_chunks
        class_rows = seq // (n_chunks * n_classes)
        static = isinstance(n, int)

        def chunks(n):
            starts = [(n_classes * c + cls) * class_rows + n * chunk_len for c in range(n_chunks)]
            return [pl.ds(s if static else pl.multiple_of(s, chunk_len), chunk_len)
                    for s in starts]

        def load(ref, slices):
            parts = [ref[s, :] for s in slices]
            return parts[0] if len(parts) == 1 else jnp.concatenate(parts, axis=0)

        def store(ref, slices, val):
            for c, s in enumerate(slices):
                part = val[c * chunk_len:(c + 1) * chunk_len]
                if multi:
                    ref[pi, s, :] = part
                else:
                    ref[s, :] = part.astype(ref.dtype)

        if static:
            first, prev_n = int(n == 0), max(n - 1, 0)
        else:
            first, prev_n = (n == 0).astype(jnp.int32), jnp.maximum(n - 1, 0)
        cur = chunks(n)
        prev = chunks(prev_n)
        q = load(q_ref, cur).astype(BF16)
        kk = jnp.concatenate([load(k_ref, prev), load(k_ref, cur)], axis=0).astype(BF16)
        vv = jnp.concatenate([load(v_ref, prev), load(v_ref, cur)], axis=0).astype(BF16)
        q2 = jnp.concatenate([q * head_sel[h] for h in range(HEADS_PER_BLOCK)], axis=0)
        s2 = lax.dot_general(q2, kk, (((1,), (1,)), ((), ())),
                             preferred_element_type=F32)
        ps, ms = [], []
        for h in range(HEADS_PER_BLOCK):
            s = s2[h * BLOCK:(h + 1) * BLOCK]
            if fold:
                s = fold_keys(s) + bias_ref[(pi * 2 + h) * 2 + first, :, :BLOCK]
                m = jnp.max(s, axis=-1, keepdims=True)
                p = jnp.exp2(s - m).astype(BF16)
                ps.append(jnp.concatenate([p * prev_sel, p * own_sel], axis=1))
            else:
                s = s + bias_ref[(pi * 2 + h) * 2 + first]
                m = jnp.max(s, axis=-1, keepdims=True)
                ps.append(jnp.exp2(s - m).astype(BF16))
            ms.append(m)
        rhs = jnp.concatenate(
            [jnp.concatenate([vv * sel2[h], sel2[h]], axis=1) for h in range(HEADS_PER_BLOCK)],
            axis=0)
        nd = jnp.dot(jnp.concatenate(ps, axis=1), rhs, preferred_element_type=F32)
        num, den = nd[:, :LANES], nd[:, LANES:]
        m2 = jnp.where(low_half, ms[0], ms[1])
        if multi:
            store(acc_ref, cur, num)
            store(max_ref, cur, m2)
            store(den_ref, cur, den)
        else:
            if use_sinks:
                den = den + jnp.exp2(sink2 - m2)
            store(o_ref, cur, num / den)

    def combine(row0, n_rows):
        rows = pl.ds(row0, n_rows)
        mxs = [max_ref[pi, rows, :] for pi in range(len(patterns))]
        mx = functools.reduce(jnp.maximum, mxs)
        es = [jnp.exp2(m - mx) for m in mxs]
        add = lambda a, b: a + b
        num = functools.reduce(add, [e * acc_ref[pi, rows, :] for pi, e in enumerate(es)])
        tot = functools.reduce(add, [e * den_ref[pi, rows, :] for pi, e in enumerate(es)])
        o_ref[rows, :] = (num / tot).astype(o_ref.dtype)

    for pi, (n_chunks, n_classes, _, _) in enumerate(patterns):
        class_rows = seq // (n_chunks * n_classes)
        blocks_per_class = class_rows // (BLOCK // n_chunks)
        shift = blocks_per_class.bit_length() - 1
        assert blocks_per_class == 1 << shift and n_classes * blocks_per_class == seq // BLOCK
        if multi and pi == len(patterns) - 1:
            assert n_chunks == 1
            for cls in range(n_classes):
                for n in range(blocks_per_class):
                    block_task(pi, cls, n)
                combine(cls * class_rows, class_rows)
        else:
            def task(t, carry, pi=pi, shift=shift, blocks_per_class=blocks_per_class):
                block_task(pi, lax.shift_right_logical(t, shift),
                           lax.bitwise_and(t, blocks_per_class - 1))
                return carry

            lax.fori_loop(0, seq // BLOCK, task, 0, unroll=True)


def _attention(q_arr, kv_arr, slopes, sinks, *, q_block, k_block, v_block, patterns, use_sinks):
    b, seq, _ = q_arr.shape
    n_blocks = N_HEADS // HEADS_PER_BLOCK
    n_pat = len(patterns)
    scratch = [pltpu.VMEM((n_pat * 4, BLOCK, 2 * BLOCK), F32)]
    if n_pat > 1:
        scratch += [pltpu.VMEM((n_pat, seq, LANES), F32) for _ in range(3)]
    smem = pl.BlockSpec(memory_space=pltpu.SMEM)
    return pl.pallas_call(
        functools.partial(_attn_kernel, patterns=patterns, use_sinks=use_sinks),
        grid=(n_blocks, b),
        in_specs=[
            smem, smem,
            pl.BlockSpec((None, seq, LANES), lambda j, i: (i, 0, q_block(j))),
            pl.BlockSpec((None, seq, LANES), lambda j, i: (i, 0, k_block(j))),
            pl.BlockSpec((None, seq, LANES), lambda j, i: (i, 0, v_block(j))),
        ],
        out_specs=pl.BlockSpec((None, seq, LANES), lambda j, i: (i, 0, j)),
        out_shape=jax.ShapeDtypeStruct((b, seq, D_MODEL), BF16),
        scratch_shapes=scratch,
        compiler_params=pltpu.CompilerParams(
            dimension_semantics=("arbitrary", "arbitrary"), vmem_limit_bytes=VMEM_LIMIT),
        name="attn_dilated" if n_pat > 1 else "attn_swa",
    )(slopes, sinks, q_arr, kv_arr, kv_arr)


def _alibi_slopes(n):
    return np.array([2.0 ** (-8.0 * (h + 1) / n) for h in range(n)], dtype=np.float32)


def kernel(x, ffn1_w_in, ffn1_w_out, ffn2_w_in, ffn2_w_out, ln_g, ln_b,
           a_w_qkv, a_w_o, kv_w, b_w_q, b_sinks, b_w_o):
    b, seq, d = x.shape
    t = b * seq
    assert x.dtype == F32 and d == D_MODEL and ffn1_w_in.shape[-1] == 2 * D_FF
    assert seq % (MAX_DILATION * BLOCK) == 0 and seq % ROW_TILE == 0 and t % PROJ_ROWS == 0
    assert all(n % W_STEPS == 0 for n in (d, D_FF)) and (D_FF // W_STEPS) % 16 == 0
    xs = x.reshape(t, d)
    slopes = _alibi_slopes(N_HEADS)
    no_sinks = jnp.zeros((N_HEADS,), F32)
    dil_patterns = tuple((MAX_DILATION // dil, dil, win // dil, float(dil))
                         for win, dil in DILATED_PATTERNS)
    swa_patterns = ((1, 1, WINDOW_B - 1, 1.0),)
    n_blocks = N_HEADS // HEADS_PER_BLOCK
    head_order = np.array([_swa_head(p) for p in range(N_HEADS)], dtype=np.int32)
    kv_blocks = N_KV_B // HEADS_PER_BLOCK
    kv = None
    order = "nat"
    for i in range(DEPTH):
        dilated = i < N_A_LAYERS
        j = i - N_A_LAYERS
        w_in, w_out = (ffn1_w_in, i, "gate_up"), (ffn1_w_out, i, None)
        if dilated:
            xs = _block(xs, w_in, w_out, ln_g[i, 0], ln_b[i, 0], seq=seq,
                        in_order=order, out_order="cm")
            order = "cm"
            qkv = _proj(xs, (a_w_qkv, i, None), out_dtype=F32, n_scaled=d)
            qkv = qkv.reshape(b, seq, 3 * d)
            att = _attention(
                qkv, qkv, jnp.asarray(slopes), no_sinks,
                q_block=lambda jb: jb, k_block=lambda jb: n_blocks + jb,
                v_block=lambda jb: 2 * n_blocks + jb,
                patterns=dil_patterns, use_sinks=False)
            w_o = (a_w_o, i, None)
        else:
            xs, q = _block(xs, w_in, w_out, ln_g[i, 0], ln_b[i, 0], seq=seq,
                           in_order=order, out_order="nat",
                           proj=((b_w_q, j, "head_cols"), Q_SCALE, BF16))
            order = "nat"
            att = _attention(
                q.reshape(b, seq, d), kv, jnp.asarray(slopes[head_order]),
                b_sinks[j][head_order],
                q_block=lambda jb: jb, k_block=lambda jb: jb // GROUP_B,
                v_block=lambda jb: kv_blocks + jb // GROUP_B,
                patterns=swa_patterns, use_sinks=True)
            w_o = (b_w_o, j, "head_rows")
        share_kv = i == N_A_LAYERS - 1
        out = _block(xs, (ffn2_w_in, i, "gate_up"), (ffn2_w_out, i, None),
                     ln_g[i, 2], ln_b[i, 2], seq=seq, in_order=order, out_order="nat",
                     attn=(att.reshape(t, d), w_o, ln_g[i, 1], ln_b[i, 1]),
                     proj=((kv_w, None, None), None, BF16) if share_kv else None)
        order = "nat"
        if share_kv:
            xs, kv = out
            kv = kv.reshape(b, seq, 2 * N_KV_B * HEAD_DIM)
        else:
            xs = out
    return xs.reshape(b, seq, d)
```

```python
import functools
import math

import numpy as np
import jax
import jax.numpy as jnp
from jax import lax
from jax.experimental import pallas as pl
from jax.experimental.pallas import tpu as pltpu

D_MODEL = 1024
DEPTH = 2
HEAD_DIM = 64
N_HEADS = D_MODEL // HEAD_DIM
N_KV_B = 4
GROUP_B = N_HEADS // N_KV_B
D_FF = 2816
DILATED_PATTERNS = ((128, 1), (512, 4), (2048, 16))
MAX_DILATION = max(d for _, d in DILATED_PATTERNS)
WINDOW_B = 128
BLOCK = 128
N_A_LAYERS = DEPTH // 2
ALPHA = (2.0 * DEPTH) ** 0.25
LN_EPS = 1e-5

LANES = 128
HEADS_PER_BLOCK = LANES // HEAD_DIM
MASK_VALUE = -1e30
LOG2E = math.log2(math.e)
Q_SCALE = HEAD_DIM ** -0.5 * LOG2E
VMEM_LIMIT = 56 * 1024 * 1024
ROW_TILE = 512
PROJ_ROWS = 1024

BF16 = jnp.bfloat16
F32 = jnp.float32


def _layer_norm(z, g, b):
    mu = jnp.mean(z, axis=-1, keepdims=True)
    zc = z - mu
    var = jnp.mean(zc * zc, axis=-1, keepdims=True)
    return zc * lax.rsqrt(var + LN_EPS) * g + b


def _resident(shape):
    return pl.BlockSpec(shape, lambda *_: (0,) * len(shape), pipeline_mode=pl.Buffered(1))


CLASS_ROWS = ROW_TILE // MAX_DILATION
N_SLABS = D_MODEL // LANES
W_STEPS = 16


def _tile_spec(order, seq, w, n_tiles, lag=0, rows=ROW_TILE):
    tile = lambda i: jnp.clip(i - W_STEPS - lag, 0, n_tiles - 1)
    if order == "nat":
        return pl.BlockSpec((rows, w), lambda i: (tile(i), 0))
    assert rows == ROW_TILE
    tiles_per_batch = seq // ROW_TILE

    def index(i):
        i = tile(i)
        return (i // tiles_per_batch, 0, i % tiles_per_batch, 0)

    return pl.BlockSpec((None, MAX_DILATION, CLASS_ROWS, w), index)


def _tile_view(x, order, seq):
    t, w = x.shape
    if order == "nat":
        return x
    return x.reshape(t // seq, MAX_DILATION, seq // MAX_DILATION, w)


def _swa_head(p):
    per = HEADS_PER_BLOCK * GROUP_B
    return (HEADS_PER_BLOCK * (p // per) + p % HEADS_PER_BLOCK) * GROUP_B + (p % per) // HEADS_PER_BLOCK


def _weight_chunk_spec(w, layer, layout):
    rows, cols = w.shape[-2] // W_STEPS, w.shape[-1]
    chunk = lambda i: jnp.minimum(i, W_STEPS - 1)
    if layout == "head_rows":
        assert rows == HEAD_DIM and W_STEPS == N_HEADS
        chunk = lambda i: _swa_head(jnp.minimum(i, W_STEPS - 1))
    if layer is None:
        return pl.BlockSpec((rows, cols), lambda i: (chunk(i), 0))
    return pl.BlockSpec((None, rows, cols), lambda i: (layer, chunk(i), 0))


def _weight_scratch(w):
    return pltpu.VMEM(w.shape[-2:], BF16)


def _load_weight_chunks(step, entries):
    for src_ref, dst_ref, layout in entries:
        rows, cols = src_ref.shape
        dst_rows = pl.ds(pl.multiple_of(step * rows, rows), rows)
        if layout == "gate_up":
            half = cols // 2
            for k in range(half // LANES):
                lanes = slice(k * LANES, (k + 1) * LANES)
                dst_ref[dst_rows, 2 * k * LANES:(2 * k + 1) * LANES] = (
                    src_ref[:, lanes].astype(BF16))
                dst_ref[dst_rows, (2 * k + 1) * LANES:(2 * k + 2) * LANES] = (
                    src_ref[:, half + k * LANES:half + (k + 1) * LANES].astype(BF16))
        elif layout == "head_cols":
            for blk in range(cols // LANES):
                heads = [_swa_head(HEADS_PER_BLOCK * blk + e) for e in range(HEADS_PER_BLOCK)]
                dst_ref[dst_rows, blk * LANES:(blk + 1) * LANES] = jnp.concatenate(
                    [src_ref[:, h * HEAD_DIM:(h + 1) * HEAD_DIM] for h in heads],
                    axis=1).astype(BF16)
        else:
            dst_ref[dst_rows, :] = src_ref[...].astype(BF16)


SUB = 4
SUB_ROWS = ROW_TILE // SUB
assert SUB * SUB == MAX_DILATION


def _block_kernel(*refs, has_attn, has_proj, proj_scale, proj_layout, in_order, out_order):
    refs = list(refs)
    x_ref = refs.pop(0)
    if has_attn:
        a_ref, wo_src, g1_ref, b1_ref = refs[:4]
        del refs[:4]
    win_src, wout_src, g2_ref, b2_ref = refs[:4]
    del refs[:4]
    wp_src = refs.pop(0) if has_proj else None
    o_ref = refs.pop(0)
    p_ref = refs.pop(0) if has_proj else None
    win_ref, wout_ref = refs[:2]
    del refs[:2]
    weights = [(win_src, win_ref, "gate_up"), (wout_src, wout_ref, None)]
    if has_attn:
        wo_ref = refs.pop(0)
        weights.append((wo_src, wo_ref, None))
    if has_proj:
        wp_ref = refs.pop(0)
        weights.append((wp_src, wp_ref, proj_layout))
    slab_ref = refs.pop(0) if in_order != out_order else None
    z_ref = refs.pop(0)
    step = pl.program_id(0)
    last = pl.num_programs(0) - 1

    @pl.when(step < W_STEPS)
    def _():
        _load_weight_chunks(step, weights)

    @pl.when(step == 0)
    def _():
        z_ref[...] = jnp.zeros_like(z_ref)

    def compute():
        d = x_ref.shape[-1]
        x = x_ref[...].reshape(ROW_TILE, d)
        if has_attn:
            a = a_ref[...].reshape(ROW_TILE, a_ref.shape[-1])
            mix = jnp.dot(a.astype(BF16), wo_ref[...], preferred_element_type=F32)
            x = _layer_norm(ALPHA * x + mix, g1_ref[...], b1_ref[...])
        h = jnp.dot(x.astype(BF16), win_ref[...], preferred_element_type=F32)
        acts = []
        for k in range(D_FF // LANES):
            gate = h[:, 2 * k * LANES:(2 * k + 1) * LANES]
            up = h[:, (2 * k + 1) * LANES:(2 * k + 2) * LANES]
            acts.append((gate * jax.nn.sigmoid(gate) * up).astype(BF16))
        act = jnp.concatenate(acts, axis=1)
        ffn = jnp.dot(act, wout_ref[...], preferred_element_type=F32)
        z_ref[...] = ALPHA * x + 0.5 * ffn

    def finish():
        y = _layer_norm(z_ref[...], g2_ref[...], b2_ref[...])
        if in_order == out_order:
            o_ref[...] = y.reshape(o_ref.shape)
        elif out_order == "cm":
            for j in range(N_SLABS):
                slab_ref[0, j] = y[:, j * LANES:(j + 1) * LANES]
            for j in range(N_SLABS):
                for r1 in range(SUB):
                    slab_ref[1, j, r1 * SUB_ROWS:(r1 + 1) * SUB_ROWS, :] = (
                        slab_ref[0, j, pl.ds(r1, SUB_ROWS, stride=SUB), :])
            for r1 in range(SUB):
                for r2 in range(SUB):
                    o_ref[SUB * r2 + r1] = jnp.concatenate(
                        [slab_ref[1, j, pl.ds(r1 * SUB_ROWS + r2, CLASS_ROWS, stride=SUB), :]
                         for j in range(N_SLABS)], axis=1)
        else:
            for r1 in range(SUB):
                for r2 in range(SUB):
                    r = SUB * r2 + r1
                    for j in range(N_SLABS):
                        slab_ref[1, j, pl.ds(r1 * SUB_ROWS + r2, CLASS_ROWS, stride=SUB), :] = (
                            y[r * CLASS_ROWS:(r + 1) * CLASS_ROWS, j * LANES:(j + 1) * LANES])
            for j in range(N_SLABS):
                for r1 in range(SUB):
                    slab_ref[0, j, pl.ds(r1, SUB_ROWS, stride=SUB), :] = (
                        slab_ref[1, j, r1 * SUB_ROWS:(r1 + 1) * SUB_ROWS, :])
            y = jnp.concatenate([slab_ref[0, j] for j in range(N_SLABS)], axis=1)
            o_ref[...] = y
        if has_proj:
            assert out_order == "nat"
            res = jnp.dot(y.astype(BF16), wp_ref[...], preferred_element_type=F32)
            if proj_scale is not None:
                res = res * proj_scale
            p_ref[...] = res.astype(p_ref.dtype)

    @pl.when((step >= W_STEPS) & (step < last))
    def _():
        finish()
        compute()

    @pl.when(step == last)
    def _():
        finish()


def _block(x, w_in, w_out, g, b, *, seq, in_order, out_order, attn=None, proj=None):
    t, d = x.shape
    n_tiles = t // ROW_TILE
    vec = lambda v: v.reshape(1, d)
    in_spec = lambda w: _tile_spec(in_order, seq, w, n_tiles)
    out_spec = lambda order, w: _tile_spec(order, seq, w, n_tiles, lag=1)
    args, specs = [_tile_view(x, in_order, seq)], [in_spec(d)]
    scratch = [_weight_scratch(w_in[0]), _weight_scratch(w_out[0])]
    if attn is not None:
        att, w_o, g1, b1 = attn
        args += [_tile_view(att, in_order, seq), w_o[0], vec(g1), vec(b1)]
        specs += [in_spec(d), _weight_chunk_spec(*w_o), _resident((1, d)), _resident((1, d))]
        scratch.append(_weight_scratch(w_o[0]))
    args += [w_in[0], w_out[0], vec(g), vec(b)]
    specs += [_weight_chunk_spec(*w_in), _weight_chunk_spec(*w_out),
              _resident((1, d)), _resident((1, d))]
    out_shapes = [jax.ShapeDtypeStruct(_tile_view(x, out_order, seq).shape, F32)]
    out_specs = [out_spec(out_order, d)]
    proj_scale = proj_layout = None
    if proj is not None:
        w_p, proj_scale, p_dtype = proj
        proj_layout = w_p[2]
        n_p = w_p[0].shape[-1]
        args.append(w_p[0])
        specs.append(_weight_chunk_spec(*w_p))
        scratch.append(_weight_scratch(w_p[0]))
        out_shapes.append(jax.ShapeDtypeStruct((t, n_p), p_dtype))
        out_specs.append(out_spec("nat", n_p))
    if in_order != out_order:
        scratch.append(pltpu.VMEM((2, N_SLABS, ROW_TILE, LANES), F32))
    scratch.append(pltpu.VMEM((ROW_TILE, d), F32))
    outs = pl.pallas_call(
        functools.partial(_block_kernel, has_attn=attn is not None, has_proj=proj is not None,
                          proj_scale=proj_scale, proj_layout=proj_layout,
                          in_order=in_order, out_order=out_order),
        grid=(W_STEPS + n_tiles + 1,),
        in_specs=specs,
        out_specs=out_specs,
        out_shape=out_shapes,
        scratch_shapes=scratch,
        compiler_params=pltpu.CompilerParams(
            dimension_semantics=("arbitrary",), vmem_limit_bytes=VMEM_LIMIT),
        name="block_attn" if attn is not None else "block_ffn",
    )(*args)
    y = outs[0].reshape(t, d)
    return (y, outs[1]) if proj is not None else y


def _proj_kernel(x_ref, w_src, o_ref, w_ref, *, n_scaled):
    step = pl.program_id(0)

    @pl.when(step < W_STEPS)
    def _():
        _load_weight_chunks(step, [(w_src, w_ref, None)])

    @pl.when(step >= W_STEPS)
    def _():
        res = jnp.dot(x_ref[...].astype(BF16), w_ref[...], preferred_element_type=F32)
        if n_scaled:
            col = lax.broadcasted_iota(jnp.int32, (1, res.shape[1]), 1)
            res = res * jnp.where(col < n_scaled, Q_SCALE, 1.0)
        o_ref[...] = res.astype(o_ref.dtype)


def _proj(x, w, *, out_dtype, n_scaled=0):
    t, d = x.shape
    n = w[0].shape[-1]
    n_tiles = t // PROJ_ROWS
    return pl.pallas_call(
        functools.partial(_proj_kernel, n_scaled=n_scaled),
        grid=(W_STEPS + n_tiles,),
        in_specs=[_tile_spec("nat", 0, d, n_tiles, rows=PROJ_ROWS), _weight_chunk_spec(*w)],
        out_specs=_tile_spec("nat", 0, n, n_tiles, rows=PROJ_ROWS),
        out_shape=jax.ShapeDtypeStruct((t, n), out_dtype),
        scratch_shapes=[_weight_scratch(w[0])],
        compiler_params=pltpu.CompilerParams(
            dimension_semantics=("arbitrary",), vmem_limit_bytes=VMEM_LIMIT),
        name="proj",
    )(x, w[0])


def _attn_kernel(slope_ref, sink_ref, q_ref, k_ref, v_ref, o_ref, bias_ref, *scratch,
                 patterns, use_sinks):
    seq = q_ref.shape[-2]
    batches = range(q_ref.shape[0]) if len(q_ref.shape) == 3 else (None,)
    blk = pl.program_id(0)
    multi = len(patterns) > 1
    if multi:
        acc_ref, max_ref, den_ref = scratch

    lane = lax.broadcasted_iota(jnp.int32, (BLOCK, LANES), 1)
    low_half = lane < HEAD_DIM
    head_sel = [jnp.where((lane >= h * HEAD_DIM) & (lane < (h + 1) * HEAD_DIM), 1.0, 0.0).astype(BF16)
                for h in range(HEADS_PER_BLOCK)]
    sel2 = [jnp.concatenate([sel, sel], axis=0) for sel in head_sel]

    @pl.when(pl.program_id(1) == 0)
    def _():
        row = lax.broadcasted_iota(jnp.int32, (BLOCK, 2 * BLOCK), 0)
        col = lax.broadcasted_iota(jnp.int32, (BLOCK, 2 * BLOCK), 1)
        in_cur = col >= BLOCK
        for pi, (n_chunks, _, max_dist, scale) in enumerate(patterns):
            chunk_len = BLOCK // n_chunks
            sh = chunk_len.bit_length() - 1
            pos = lambda x: n_chunks * (x & (chunk_len - 1)) + (x >> sh)
            dist = pos(row) - pos(col & (BLOCK - 1)) + jnp.where(in_cur, 0, BLOCK)
            valid = (dist >= 0) & (dist <= max_dist)
            for h in range(HEADS_PER_BLOCK):
                slope = slope_ref[blk * HEADS_PER_BLOCK + h]
                bias = jnp.where(valid, (-slope * scale * LOG2E) * dist.astype(F32), MASK_VALUE)
                bias_ref[(pi * 2 + h) * 2 + 0] = bias
                bias_ref[(pi * 2 + h) * 2 + 1] = jnp.where(in_cur, bias, MASK_VALUE)

    if use_sinks:
        sink2 = jnp.where(low_half, sink_ref[blk * HEADS_PER_BLOCK] * LOG2E,
                          sink_ref[blk * HEADS_PER_BLOCK + 1] * LOG2E)

    def block_task(pi, cls, n, bi=None):
        n_chunks, n_classes, _, _ = patterns[pi]
        chunk_len = BLOCK // n_chunks
        class_rows = seq // (n_chunks * n_classes)
        static = isinstance(n, int)

        def chunks(n):
            starts = [(n_classes * c + cls) * class_rows + n * chunk_len for c in range(n_chunks)]
            return [pl.ds(s if static else pl.multiple_of(s, chunk_len), chunk_len)
                    for s in starts]

        def load(ref, slices):
            parts = [ref[s, :] if bi is None else ref[bi, s, :] for s in slices]
            return parts[0] if len(parts) == 1 else jnp.concatenate(parts, axis=0)

        def store(ref, slices, val):
            for c, s in enumerate(slices):
                part = val[c * chunk_len:(c + 1) * chunk_len]
                if multi:
                    ref[pi, s, :] = part
                elif bi is None:
                    ref[s, :] = part.astype(ref.dtype)
                else:
                    ref[bi, s, :] = part.astype(ref.dtype)

        if static:
            first, prev_n = int(n == 0), max(n - 1, 0)
        else:
            first, prev_n = (n == 0).astype(jnp.int32), jnp.maximum(n - 1, 0)
        cur = chunks(n)
        prev = chunks(prev_n)
        q = load(q_ref, cur).astype(BF16)
        kk = jnp.concatenate([load(k_ref, prev), load(k_ref, cur)], axis=0).astype(BF16)
        vv = jnp.concatenate([load(v_ref, prev), load(v_ref, cur)], axis=0).astype(BF16)
        q2 = jnp.concatenate([q * head_sel[h] for h in range(HEADS_PER_BLOCK)], axis=0)
        s2 = lax.dot_general(q2, kk, (((1,), (1,)), ((), ())),
                             preferred_element_type=F32)
        ps, ms = [], []
        for h in range(HEADS_PER_BLOCK):
            s = s2[h * BLOCK:(h + 1) * BLOCK] + bias_ref[(pi * 2 + h) * 2 + first]
            m = jnp.max(s, axis=-1, keepdims=True)
            ps.append(jnp.exp2(s - m).astype(BF16))
            ms.append(m)
        rhs = jnp.concatenate(
            [jnp.concatenate([vv * sel2[h], sel2[h]], axis=1) for h in range(HEADS_PER_BLOCK)],
            axis=0)
        nd = jnp.dot(jnp.concatenate(ps, axis=1), rhs, preferred_element_type=F32)
        num, den = nd[:, :LANES], nd[:, LANES:]
        m2 = jnp.where(low_half, ms[0], ms[1])
        if multi:
            store(acc_ref, cur, num)
            store(max_ref, cur, m2)
            store(den_ref, cur, den)
        else:
            if use_sinks:
                den = den + jnp.exp2(sink2 - m2)
            store(o_ref, cur, num / den)

    def combine(row0, n_rows):
        rows = pl.ds(row0, n_rows)
        mxs = [max_ref[pi, rows, :] for pi in range(len(patterns))]
        mx = functools.reduce(jnp.maximum, mxs)
        es = [jnp.exp2(m - mx) for m in mxs]
        add = lambda a, b: a + b
        num = functools.reduce(add, [e * acc_ref[pi, rows, :] for pi, e in enumerate(es)])
        tot = functools.reduce(add, [e * den_ref[pi, rows, :] for pi, e in enumerate(es)])
        o_ref[rows, :] = (num / tot).astype(o_ref.dtype)

    for pi, (n_chunks, n_classes, _, _) in enumerate(patterns):
        class_rows = seq // (n_chunks * n_classes)
        blocks_per_class = class_rows // (BLOCK // n_chunks)
        shift = blocks_per_class.bit_length() - 1
        assert blocks_per_class == 1 << shift and n_classes * blocks_per_class == seq // BLOCK
        if multi and pi == len(patterns) - 1:
            assert n_chunks == 1
            for cls in range(n_classes):
                for n in range(blocks_per_class):
                    block_task(pi, cls, n)
                combine(cls * class_rows, class_rows)
        else:
            for bi in batches:
                def task(t, carry, pi=pi, shift=shift, blocks_per_class=blocks_per_class, bi=bi):
                    block_task(pi, lax.shift_right_logical(t, shift),
                               lax.bitwise_and(t, blocks_per_class - 1), bi)
                    return carry

                lax.fori_loop(0, seq // BLOCK, task, 0, unroll=True)


def _attention(q_arr, kv_arr, slopes, sinks, *, q_block, k_block, v_block, patterns, use_sinks):
    b, seq, _ = q_arr.shape
    n_blocks = N_HEADS // HEADS_PER_BLOCK
    n_pat = len(patterns)
    scratch = [pltpu.VMEM((n_pat * 4, BLOCK, 2 * BLOCK), F32)]
    if n_pat > 1:
        scratch += [pltpu.VMEM((n_pat, seq, LANES), F32) for _ in range(3)]
    smem = pl.BlockSpec(memory_space=pltpu.SMEM)
    nb = None if n_pat > 1 or b % 2 else 2
    window = lambda col: pl.BlockSpec((nb, seq, LANES), lambda j, i: (i, 0, col(j)))
    return pl.pallas_call(
        functools.partial(_attn_kernel, patterns=patterns, use_sinks=use_sinks),
        grid=(n_blocks, b // (nb or 1)),
        in_specs=[smem, smem, window(q_block), window(k_block), window(v_block)],
        out_specs=window(lambda j: j),
        out_shape=jax.ShapeDtypeStruct((b, seq, D_MODEL), BF16),
        scratch_shapes=scratch,
        compiler_params=pltpu.CompilerParams(
            dimension_semantics=("arbitrary", "arbitrary"), vmem_limit_bytes=VMEM_LIMIT),
        name="attn_dilated" if n_pat > 1 else "attn_swa",
    )(slopes, sinks, q_arr, kv_arr, kv_arr)


def _alibi_slopes(n):
    return np.array([2.0 ** (-8.0 * (h + 1) / n) for h in range(n)], dtype=np.float32)


def kernel(x, ffn1_w_in, ffn1_w_out, ffn2_w_in, ffn2_w_out, ln_g, ln_b,
           a_w_qkv, a_w_o, kv_w, b_w_q, b_sinks, b_w_o):
    b, seq, d = x.shape
    t = b * seq
    assert x.dtype == F32 and d == D_MODEL and ffn1_w_in.shape[-1] == 2 * D_FF
    assert seq % (MAX_DILATION * BLOCK) == 0 and seq % ROW_TILE == 0 and t % PROJ_ROWS == 0
    assert all(n % W_STEPS == 0 for n in (d, D_FF)) and (D_FF // W_STEPS) % 16 == 0
    xs = x.reshape(t, d)
    slopes = _alibi_slopes(N_HEADS)
    no_sinks = jnp.zeros((N_HEADS,), F32)
    dil_patterns = tuple((MAX_DILATION // dil, dil, win // dil, float(dil))
                         for win, dil in DILATED_PATTERNS)
    swa_patterns = ((1, 1, WINDOW_B - 1, 1.0),)
    n_blocks = N_HEADS // HEADS_PER_BLOCK
    head_order = np.array([_swa_head(p) for p in range(N_HEADS)], dtype=np.int32)
    kv_blocks = N_KV_B // HEADS_PER_BLOCK
    kv = None
    order = "nat"
    for i in range(DEPTH):
        dilated = i < N_A_LAYERS
        j = i - N_A_LAYERS
        w_in, w_out = (ffn1_w_in, i, "gate_up"), (ffn1_w_out, i, None)
        if dilated:
            xs = _block(xs, w_in, w_out, ln_g[i, 0], ln_b[i, 0], seq=seq,
                        in_order=order, out_order="cm")
            order = "cm"
            qkv = _proj(xs, (a_w_qkv, i, None), out_dtype=F32, n_scaled=d)
            qkv = qkv.reshape(b, seq, 3 * d)
            att = _attention(
                qkv, qkv, jnp.asarray(slopes), no_sinks,
                q_block=lambda jb: jb, k_block=lambda jb: n_blocks + jb,
                v_block=lambda jb: 2 * n_blocks + jb,
                patterns=dil_patterns, use_sinks=False)
            w_o = (a_w_o, i, None)
        else:
            xs, q = _block(xs, w_in, w_out, ln_g[i, 0], ln_b[i, 0], seq=seq,
                           in_order=order, out_order="nat",
                           proj=((b_w_q, j, "head_cols"), Q_SCALE, BF16))
            order = "nat"
            att = _attention(
                q.reshape(b, seq, d), kv, jnp.asarray(slopes[head_order]),
                b_sinks[j][head_order],
                q_block=lambda jb: jb, k_block=lambda jb: jb // GROUP_B,
                v_block=lambda jb: kv_blocks + jb // GROUP_B,
                patterns=swa_patterns, use_sinks=True)
            w_o = (b_w_o, j, "head_rows")
        share_kv = i == N_A_LAYERS - 1
        out = _block(xs, (ffn2_w_in, i, "gate_up"), (ffn2_w_out, i, None),
                     ln_g[i, 2], ln_b[i, 2], seq=seq, in_order=order, out_order="nat",
                     attn=(att.reshape(t, d), w_o, ln_g[i, 1], ln_b[i, 1]),
                     proj=((kv_w, None, None), None, BF16) if share_kv else None)
        order = "nat"
        if share_kv:
            xs, kv = out
            kv = kv.reshape(b, seq, 2 * N_KV_B * HEAD_DIM)
        else:
            xs = out
    return xs.reshape(b, seq, d)
```

```python
import functools
import math

import numpy as np
import jax
import jax.numpy as jnp
from jax import lax
from jax.experimental import pallas as pl
from jax.experimental.pallas import tpu as pltpu

D_MODEL = 1024
DEPTH = 2
HEAD_DIM = 64
N_HEADS = D_MODEL // HEAD_DIM
N_KV_B = 4
GROUP_B = N_HEADS // N_KV_B
D_FF = 2816
DILATED_PATTERNS = ((128, 1), (512, 4), (2048, 16))
MAX_DILATION = max(d for _, d in DILATED_PATTERNS)
WINDOW_B = 128
BLOCK = 128
N_A_LAYERS = DEPTH // 2
ALPHA = (2.0 * DEPTH) ** 0.25
LN_EPS = 1e-5

LANES = 128
HEADS_PER_BLOCK = LANES // HEAD_DIM
MASK_VALUE = -1e30
LOG2E = math.log2(math.e)
Q_SCALE = HEAD_DIM ** -0.5 * LOG2E
VMEM_LIMIT = 56 * 1024 * 1024
ROW_TILE = 512
PROJ_ROWS = 1024

BF16 = jnp.bfloat16
F32 = jnp.float32


def _layer_norm(z, g, b):
    mu = jnp.mean(z, axis=-1, keepdims=True)
    zc = z - mu
    var = jnp.mean(zc * zc, axis=-1, keepdims=True)
    return zc * lax.rsqrt(var + LN_EPS) * g + b


def _resident(shape):
    return pl.BlockSpec(shape, lambda *_: (0,) * len(shape), pipeline_mode=pl.Buffered(1))


CLASS_ROWS = ROW_TILE // MAX_DILATION
N_SLABS = D_MODEL // LANES
W_STEPS = 16


def _tile_spec(order, seq, w, n_tiles, lag=0, rows=ROW_TILE):
    tile = lambda i: jnp.clip(i - W_STEPS - lag, 0, n_tiles - 1)
    if order == "nat":
        return pl.BlockSpec((rows, w), lambda i: (tile(i), 0))
    assert rows == ROW_TILE
    tiles_per_batch = seq // ROW_TILE

    def index(i):
        i = tile(i)
        return (i // tiles_per_batch, 0, i % tiles_per_batch, 0)

    return pl.BlockSpec((None, MAX_DILATION, CLASS_ROWS, w), index)


def _tile_view(x, order, seq):
    t, w = x.shape
    if order == "nat":
        return x
    return x.reshape(t // seq, MAX_DILATION, seq // MAX_DILATION, w)


def _swa_head(p):
    per = HEADS_PER_BLOCK * GROUP_B
    return (HEADS_PER_BLOCK * (p // per) + p % HEADS_PER_BLOCK) * GROUP_B + (p % per) // HEADS_PER_BLOCK


def _weight_chunk_spec(w, layer, layout):
    rows, cols = w.shape[-2] // W_STEPS, w.shape[-1]
    chunk = lambda i: jnp.minimum(i, W_STEPS - 1)
    if layout == "head_rows":
        assert rows == HEAD_DIM and W_STEPS == N_HEADS
        chunk = lambda i: _swa_head(jnp.minimum(i, W_STEPS - 1))
    if layer is None:
        return pl.BlockSpec((rows, cols), lambda i: (chunk(i), 0))
    return pl.BlockSpec((None, rows, cols), lambda i: (layer, chunk(i), 0))


def _weight_scratch(w):
    return pltpu.VMEM(w.shape[-2:], BF16)


def _load_weight_chunks(step, entries):
    for src_ref, dst_ref, layout in entries:
        rows, cols = src_ref.shape
        dst_rows = pl.ds(pl.multiple_of(step * rows, rows), rows)
        if layout == "gate_up":
            half = cols // 2
            for k in range(half // LANES):
                lanes = slice(k * LANES, (k + 1) * LANES)
                dst_ref[dst_rows, 2 * k * LANES:(2 * k + 1) * LANES] = (
                    src_ref[:, lanes].astype(BF16))
                dst_ref[dst_rows, (2 * k + 1) * LANES:(2 * k + 2) * LANES] = (
                    src_ref[:, half + k * LANES:half + (k + 1) * LANES].astype(BF16))
        elif layout == "head_cols":
            for blk in range(cols // LANES):
                heads = [_swa_head(HEADS_PER_BLOCK * blk + e) for e in range(HEADS_PER_BLOCK)]
                dst_ref[dst_rows, blk * LANES:(blk + 1) * LANES] = jnp.concatenate(
                    [src_ref[:, h * HEAD_DIM:(h + 1) * HEAD_DIM] for h in heads],
                    axis=1).astype(BF16)
        else:
            dst_ref[dst_rows, :] = src_ref[...].astype(BF16)


SUB = 4
SUB_ROWS = ROW_TILE // SUB
assert SUB * SUB == MAX_DILATION


def _block_kernel(*refs, has_attn, has_proj, proj_scale, proj_layout, in_order, out_order):
    refs = list(refs)
    x_ref = refs.pop(0)
    if has_attn:
        a_ref, wo_src, g1_ref, b1_ref = refs[:4]
        del refs[:4]
    win_src, wout_src, g2_ref, b2_ref = refs[:4]
    del refs[:4]
    wp_src = refs.pop(0) if has_proj else None
    o_ref = refs.pop(0)
    p_ref = refs.pop(0) if has_proj else None
    win_ref, wout_ref = refs[:2]
    del refs[:2]
    weights = [(win_src, win_ref, "gate_up"), (wout_src, wout_ref, None)]
    if has_attn:
        wo_ref = refs.pop(0)
        weights.append((wo_src, wo_ref, None))
    if has_proj:
        wp_ref = refs.pop(0)
        weights.append((wp_src, wp_ref, proj_layout))
    slab_ref = refs.pop(0) if in_order != out_order else None
    z_ref = refs.pop(0)
    step = pl.program_id(0)
    last = pl.num_programs(0) - 1

    @pl.when(step < W_STEPS)
    def _():
        _load_weight_chunks(step, weights)

    @pl.when(step == 0)
    def _():
        z_ref[...] = jnp.zeros_like(z_ref)

    def compute():
        d = x_ref.shape[-1]
        x = x_ref[...].reshape(ROW_TILE, d)
        if has_attn:
            a = a_ref[...].reshape(ROW_TILE, a_ref.shape[-1])
            mix = jnp.dot(a.astype(BF16), wo_ref[...], preferred_element_type=F32)
            x = _layer_norm(ALPHA * x + mix, g1_ref[...], b1_ref[...])
        h = jnp.dot(x.astype(BF16), win_ref[...], preferred_element_type=F32)
        acts = []
        for k in range(D_FF // LANES):
            gate = h[:, 2 * k * LANES:(2 * k + 1) * LANES]
            up = h[:, (2 * k + 1) * LANES:(2 * k + 2) * LANES]
            acts.append((gate * jax.nn.sigmoid(gate) * up).astype(BF16))
        act = jnp.concatenate(acts, axis=1)
        ffn = jnp.dot(act, wout_ref[...], preferred_element_type=F32)
        z_ref[...] = ALPHA * x + 0.5 * ffn

    def finish():
        y = _layer_norm(z_ref[...], g2_ref[...], b2_ref[...])
        if in_order == out_order:
            o_ref[...] = y.reshape(o_ref.shape)
        elif out_order == "cm":
            for j in range(N_SLABS):
                slab_ref[0, j] = y[:, j * LANES:(j + 1) * LANES]
            for j in range(N_SLABS):
                for r1 in range(SUB):
                    slab_ref[1, j, r1 * SUB_ROWS:(r1 + 1) * SUB_ROWS, :] = (
                        slab_ref[0, j, pl.ds(r1, SUB_ROWS, stride=SUB), :])
            for r1 in range(SUB):
                for r2 in range(SUB):
                    o_ref[SUB * r2 + r1] = jnp.concatenate(
                        [slab_ref[1, j, pl.ds(r1 * SUB_ROWS + r2, CLASS_ROWS, stride=SUB), :]
                         for j in range(N_SLABS)], axis=1)
        else:
            for r1 in range(SUB):
                for r2 in range(SUB):
                    r = SUB * r2 + r1
                    for j in range(N_SLABS):
                        slab_ref[1, j, pl.ds(r1 * SUB_ROWS + r2, CLASS_ROWS, stride=SUB), :] = (
                            y[r * CLASS_ROWS:(r + 1) * CLASS_ROWS, j * LANES:(j + 1) * LANES])
            for j in range(N_SLABS):
                for r1 in range(SUB):
                    slab_ref[0, j, pl.ds(r1, SUB_ROWS, stride=SUB), :] = (
                        slab_ref[1, j, r1 * SUB_ROWS:(r1 + 1) * SUB_ROWS, :])
            y = jnp.concatenate([slab_ref[0, j] for j in range(N_SLABS)], axis=1)
            o_ref[...] = y
        if has_proj:
            assert out_order == "nat"
            res = jnp.dot(y.astype(BF16), wp_ref[...], preferred_element_type=F32)
            if proj_scale is not None:
                res = res * proj_scale
            p_ref[...] = res.astype(p_ref.dtype)

    @pl.when((step >= W_STEPS) & (step < last))
    def _():
        finish()
        compute()

    @pl.when(step == last)
    def _():
        finish()


def _block(x, w_in, w_out, g, b, *, seq, in_order, out_order, attn=None, proj=None):
    t, d = x.shape
    n_tiles = t // ROW_TILE
    vec = lambda v: v.reshape(1, d)
    in_spec = lambda w: _tile_spec(in_order, seq, w, n_tiles)
    out_spec = lambda order, w: _tile_spec(order, seq, w, n_tiles, lag=1)
    args, specs = [_tile_view(x, in_order, seq)], [in_spec(d)]
    scratch = [_weight_scratch(w_in[0]), _weight_scratch(w_out[0])]
    if attn is not None:
        att, w_o, g1, b1 = attn
        args += [_tile_view(att, in_order, seq), w_o[0], vec(g1), vec(b1)]
        specs += [in_spec(d), _weight_chunk_spec(*w_o), _resident((1, d)), _resident((1, d))]
        scratch.append(_weight_scratch(w_o[0]))
    args += [w_in[0], w_out[0], vec(g), vec(b)]
    specs += [_weight_chunk_spec(*w_in), _weight_chunk_spec(*w_out),
              _resident((1, d)), _resident((1, d))]
    out_shapes = [jax.ShapeDtypeStruct(_tile_view(x, out_order, seq).shape, F32)]
    out_specs = [out_spec(out_order, d)]
    proj_scale = proj_layout = None
    if proj is not None:
        w_p, proj_scale, p_dtype = proj
        proj_layout = w_p[2]
        n_p = w_p[0].shape[-1]
        args.append(w_p[0])
        specs.append(_weight_chunk_spec(*w_p))
        scratch.append(_weight_scratch(w_p[0]))
        out_shapes.append(jax.ShapeDtypeStruct((t, n_p), p_dtype))
        out_specs.append(out_spec("nat", n_p))
    if in_order != out_order:
        scratch.append(pltpu.VMEM((2, N_SLABS, ROW_TILE, LANES), F32))
    scratch.append(pltpu.VMEM((ROW_TILE, d), F32))
    outs = pl.pallas_call(
        functools.partial(_block_kernel, has_attn=attn is not None, has_proj=proj is not None,
                          proj_scale=proj_scale, proj_layout=proj_layout,
                          in_order=in_order, out_order=out_order),
        grid=(W_STEPS + n_tiles + 1,),
        in_specs=specs,
        out_specs=out_specs,
        out_shape=out_shapes,
        scratch_shapes=scratch,
        compiler_params=pltpu.CompilerParams(
            dimension_semantics=("arbitrary",), vmem_limit_bytes=VMEM_LIMIT),
        name="block_attn" if attn is not None else "block_ffn",
    )(*args)
    y = outs[0].reshape(t, d)
    return (y, outs[1]) if proj is not None else y


def _proj_kernel(x_ref, w_src, o_ref, w_ref, *, n_scaled):
    step = pl.program_id(0)

    @pl.when(step < W_STEPS)
    def _():
        _load_weight_chunks(step, [(w_src, w_ref, None)])

    @pl.when(step >= W_STEPS)
    def _():
        res = jnp.dot(x_ref[...].astype(BF16), w_ref[...], preferred_element_type=F32)
        if n_scaled:
            col = lax.broadcasted_iota(jnp.int32, (1, res.shape[1]), 1)
            res = res * jnp.where(col < n_scaled, Q_SCALE, 1.0)
        o_ref[...] = res.astype(o_ref.dtype)


def _proj(x, w, *, out_dtype, n_scaled=0):
    t, d = x.shape
    n = w[0].shape[-1]
    n_tiles = t // PROJ_ROWS
    return pl.pallas_call(
        functools.partial(_proj_kernel, n_scaled=n_scaled),
        grid=(W_STEPS + n_tiles,),
        in_specs=[_tile_spec("nat", 0, d, n_tiles, rows=PROJ_ROWS), _weight_chunk_spec(*w)],
        out_specs=_tile_spec("nat", 0, n, n_tiles, rows=PROJ_ROWS),
        out_shape=jax.ShapeDtypeStruct((t, n), out_dtype),
        scratch_shapes=[_weight_scratch(w[0])],
        compiler_params=pltpu.CompilerParams(
            dimension_semantics=("arbitrary",), vmem_limit_bytes=VMEM_LIMIT),
        name="proj",
    )(x, w[0])


def _attn_kernel(slope_ref, sink_ref, q_ref, k_ref, v_ref, o_ref, bias_ref, *scratch,
                 patterns, use_sinks):
    seq = q_ref.shape[-2]
    batches = range(q_ref.shape[0]) if len(q_ref.shape) == 3 else (None,)
    blk = pl.program_id(0)
    multi = len(patterns) > 1
    if multi:
        acc_ref, max_ref, den_ref = scratch

    lane = lax.broadcasted_iota(jnp.int32, (BLOCK, LANES), 1)
    low_half = lane < HEAD_DIM
    head_sel = [jnp.where((lane >= h * HEAD_DIM) & (lane < (h + 1) * HEAD_DIM), 1.0, 0.0).astype(BF16)
                for h in range(HEADS_PER_BLOCK)]
    sel2 = [jnp.concatenate([sel, sel], axis=0) for sel in head_sel]

    @pl.when(pl.program_id(1) == 0)
    def _():
        row = lax.broadcasted_iota(jnp.int32, (BLOCK, 2 * BLOCK), 0)
        col = lax.broadcasted_iota(jnp.int32, (BLOCK, 2 * BLOCK), 1)
        in_cur = col >= BLOCK
        for pi, (n_chunks, _, max_dist, scale) in enumerate(patterns):
            chunk_len = BLOCK // n_chunks
            sh = chunk_len.bit_length() - 1
            pos = lambda x: n_chunks * (x & (chunk_len - 1)) + (x >> sh)
            dist = pos(row) - pos(col & (BLOCK - 1)) + jnp.where(in_cur, 0, BLOCK)
            valid = (dist >= 0) & (dist <= max_dist)
            for h in range(HEADS_PER_BLOCK):
                slope = slope_ref[blk * HEADS_PER_BLOCK + h]
                bias = jnp.where(valid, (-slope * scale * LOG2E) * dist.astype(F32), MASK_VALUE)
                bias_ref[(pi * 2 + h) * 2 + 0] = bias
                bias_ref[(pi * 2 + h) * 2 + 1] = jnp.where(in_cur, bias, MASK_VALUE)

    if use_sinks:
        sink2 = jnp.where(low_half, sink_ref[blk * HEADS_PER_BLOCK] * LOG2E,
                          sink_ref[blk * HEADS_PER_BLOCK + 1] * LOG2E)

    def block_task(pi, cls, n, bi=None):
        n_chunks, n_classes, _, _ = patterns[pi]
        chunk_len = BLOCK // n_chunks
        class_rows = seq // (n_chunks * n_classes)
        static = isinstance(n, int)

        def chunks(n):
            starts = [(n_classes * c + cls) * class_rows + n * chunk_len for c in range(n_chunks)]
            return [pl.ds(s if static else pl.multiple_of(s, chunk_len), chunk_len)
                    for s in starts]

        def load(ref, slices):
            parts = [ref[s, :] if bi is None else ref[bi, s, :] for s in slices]
            return parts[0] if len(parts) == 1 else jnp.concatenate(parts, axis=0)

        def store(ref, slices, val):
            for c, s in enumerate(slices):
                part = val[c * chunk_len:(c + 1) * chunk_len]
                if multi:
                    ref[pi, s, :] = part
                elif bi is None:
                    ref[s, :] = part.astype(ref.dtype)
                else:
                    ref[bi, s, :] = part.astype(ref.dtype)

        if static:
            first, prev_n = int(n == 0), max(n - 1, 0)
        else:
            first, prev_n = (n == 0).astype(jnp.int32), jnp.maximum(n - 1, 0)
        cur = chunks(n)
        prev = chunks(prev_n)
        q = load(q_ref, cur).astype(BF16)
        kk = jnp.concatenate([load(k_ref, prev), load(k_ref, cur)], axis=0).astype(BF16)
        vv = jnp.concatenate([load(v_ref, prev), load(v_ref, cur)], axis=0).astype(BF16)
        q2 = jnp.concatenate([q * head_sel[h] for h in range(HEADS_PER_BLOCK)], axis=0)
        s2 = lax.dot_general(q2, kk, (((1,), (1,)), ((), ())),
                             preferred_element_type=F32)
        ps, ms = [], []
        for h in range(HEADS_PER_BLOCK):
            s = s2[h * BLOCK:(h + 1) * BLOCK] + bias_ref[(pi * 2 + h) * 2 + first]
            m = jnp.max(s, axis=-1, keepdims=True)
            ps.append(jnp.exp2(s - m).astype(BF16))
            ms.append(m)
        rhs = jnp.concatenate(
            [jnp.concatenate([vv * sel2[h], sel2[h]], axis=1) for h in range(HEADS_PER_BLOCK)],
            axis=0)
        nd = jnp.dot(jnp.concatenate(ps, axis=1), rhs, preferred_element_type=F32)
        num, den = nd[:, :LANES], nd[:, LANES:]
        m2 = jnp.where(low_half, ms[0], ms[1])
        if multi:
            store(acc_ref, cur, num)
            store(max_ref, cur, m2)
            store(den_ref, cur, den)
        else:
            if use_sinks:
                den = den + jnp.exp2(sink2 - m2)
            store(o_ref, cur, num / den)

    def combine(row0, n_rows):
        rows = pl.ds(row0, n_rows)
        mxs = [max_ref[pi, rows, :] for pi in range(len(patterns))]
        mx = functools.reduce(jnp.maximum, mxs)
        es = [jnp.exp2(m - mx) for m in mxs]
        add = lambda a, b: a + b
        num = functools.reduce(add, [e * acc_ref[pi, rows, :] for pi, e in enumerate(es)])
        tot = functools.reduce(add, [e * den_ref[pi, rows, :] for pi, e in enumerate(es)])
        o_ref[rows, :] = (num / tot).astype(o_ref.dtype)

    for pi, (n_chunks, n_classes, _, _) in enumerate(patterns):
        class_rows = seq // (n_chunks * n_classes)
        blocks_per_class = class_rows // (BLOCK // n_chunks)
        shift = blocks_per_class.bit_length() - 1
        assert blocks_per_class == 1 << shift and n_classes * blocks_per_class == seq // BLOCK
        if multi and pi == len(patterns) - 1:
            assert n_chunks == 1
            for cls in range(n_classes):
                for n in range(blocks_per_class):
                    block_task(pi, cls, n)
                combine(cls * class_rows, class_rows)
        else:
            for bi in batches:
                def task(t, carry, pi=pi, shift=shift, blocks_per_class=blocks_per_class, bi=bi):
                    block_task(pi, lax.shift_right_logical(t, shift),
                               lax.bitwise_and(t, blocks_per_class - 1), bi)
                    return carry

                lax.fori_loop(0, seq // BLOCK, task, 0, unroll=True)


def _attention(q_arr, kv_arr, slopes, sinks, *, q_block, k_block, v_block, patterns, use_sinks):
    b, seq, _ = q_arr.shape
    n_blocks = N_HEADS // HEADS_PER_BLOCK
    n_pat = len(patterns)
    scratch = [pltpu.VMEM((n_pat * 4, BLOCK, 2 * BLOCK), F32)]
    if n_pat > 1:
        scratch += [pltpu.VMEM((n_pat, seq, LANES), F32) for _ in range(3)]
    smem = pl.BlockSpec(memory_space=pltpu.SMEM)
    nb = None if n_pat > 1 or b % 4 else 4
    window = lambda col: pl.BlockSpec((nb, seq, LANES), lambda j, i: (i, 0, col(j)))
    return pl.pallas_call(
        functools.partial(_attn_kernel, patterns=patterns, use_sinks=use_sinks),
        grid=(n_blocks, b // (nb or 1)),
        in_specs=[smem, smem, window(q_block), window(k_block), window(v_block)],
        out_specs=window(lambda j: j),
        out_shape=jax.ShapeDtypeStruct((b, seq, D_MODEL), BF16),
        scratch_shapes=scratch,
        compiler_params=pltpu.CompilerParams(
            dimension_semantics=("arbitrary", "arbitrary"), vmem_limit_bytes=VMEM_LIMIT),
        name="attn_dilated" if n_pat > 1 else "attn_swa",
    )(slopes, sinks, q_arr, kv_arr, kv_arr)


def _alibi_slopes(n):
    return np.array([2.0 ** (-8.0 * (h + 1) / n) for h in range(n)], dtype=np.float32)


def kernel(x, ffn1_w_in, ffn1_w_out, ffn2_w_in, ffn2_w_out, ln_g, ln_b,
           a_w_qkv, a_w_o, kv_w, b_w_q, b_sinks, b_w_o):
    b, seq, d = x.shape
    t = b * seq
    assert x.dtype == F32 and d == D_MODEL and ffn1_w_in.shape[-1] == 2 * D_FF
    assert seq % (MAX_DILATION * BLOCK) == 0 and seq % ROW_TILE == 0 and t % PROJ_ROWS == 0
    assert all(n % W_STEPS == 0 for n in (d, D_FF)) and (D_FF // W_STEPS) % 16 == 0
    xs = x.reshape(t, d)
    slopes = _alibi_slopes(N_HEADS)
    no_sinks = jnp.zeros((N_HEADS,), F32)
    dil_patterns = tuple((MAX_DILATION // dil, dil, win // dil, float(dil))
                         for win, dil in DILATED_PATTERNS)
    swa_patterns = ((1, 1, WINDOW_B - 1, 1.0),)
    n_blocks = N_HEADS // HEADS_PER_BLOCK
    head_order = np.array([_swa_head(p) for p in range(N_HEADS)], dtype=np.int32)
    kv_blocks = N_KV_B // HEADS_PER_BLOCK
    kv = None
    order = "nat"
    for i in range(DEPTH):
        dilated = i < N_A_LAYERS
        j = i - N_A_LAYERS
        w_in, w_out = (ffn1_w_in, i, "gate_up"), (ffn1_w_out, i, None)
        if dilated:
            xs = _block(xs, w_in, w_out, ln_g[i, 0], ln_b[i, 0], seq=seq,
                        in_order=order, out_order="cm")
            order = "cm"
            qkv = _proj(xs, (a_w_qkv, i, None), out_dtype=F32, n_scaled=d)
            qkv = qkv.reshape(b, seq, 3 * d)
            att = _attention(
                qkv, qkv, jnp.asarray(slopes), no_sinks,
                q_block=lambda jb: jb, k_block=lambda jb: n_blocks + jb,
                v_block=lambda jb: 2 * n_blocks + jb,
                patterns=dil_patterns, use_sinks=False)
            w_o = (a_w_o, i, None)
        else:
            xs, q = _block(xs, w_in, w_out, ln_g[i, 0], ln_b[i, 0], seq=seq,
                           in_order=order, out_order="nat",
                           proj=((b_w_q, j, "head_cols"), Q_SCALE, BF16))
            order = "nat"
            att = _attention(
                q.reshape(b, seq, d), kv, jnp.asarray(slopes[head_order]),
                b_sinks[j][head_order],
                q_block=lambda jb: jb, k_block=lambda jb: jb // GROUP_B,
                v_block=lambda jb: kv_blocks + jb // GROUP_B,
                patterns=swa_patterns, use_sinks=True)
            w_o = (b_w_o, j, "head_rows")
        share_kv = i == N_A_LAYERS - 1
        out = _block(xs, (ffn2_w_in, i, "gate_up"), (ffn2_w_out, i, None),
                     ln_g[i, 2], ln_b[i, 2], seq=seq, in_order=order, out_order="nat",
                     attn=(att.reshape(t, d), w_o, ln_g[i, 1], ln_b[i, 1]),
                     proj=((kv_w, None, None), None, BF16) if share_kv else None)
        order = "nat"
        if share_kv:
            xs, kv = out
            kv = kv.reshape(b, seq, 2 * N_KV_B * HEAD_DIM)
        else:
            xs = out
    return xs.reshape(b, seq, d)
```
